```python
import jax, jax.numpy as jnp
from jax import lax
import numpy as np

D_MODEL = 4096
BATCH = 4
SEQ = 2048
DEPTH = 2

MEM_LEN = 256
EPS = 1e-6
ROPE_THETA = 10000.0
NEG_INF = -1e30
QBLOCK = 128
MLA_HEADS = 16
Q_LORA = 1024
KV_LORA = 512
NOPE_DIM = 128
ROPE_DIM = 64
MLA_V_DIM = 128
MLA_QK_DIM = NOPE_DIM + ROPE_DIM
SWA_HEADS = 32
SWA_KV_HEADS = 8
SWA_HD = 64
SWA_GROUP = SWA_HEADS // SWA_KV_HEADS
WINDOW = 128
IN_WIDTHS = (Q_LORA, KV_LORA, ROPE_DIM, SWA_HEADS * SWA_HD, SWA_KV_HEADS * SWA_HD,
             SWA_KV_HEADS * SWA_HD, D_MODEL, D_MODEL)
D_IN = Q_LORA + KV_LORA + ROPE_DIM + SWA_HEADS * SWA_HD + 2 * SWA_KV_HEADS * SWA_HD + 2 * D_MODEL
X_HEADS = 4
X_HD = 128
N_GROUPS = 4
EXPERTS_PER_GROUP = 8
N_EXPERTS = N_GROUPS * EXPERTS_PER_GROUP
TOP_K = 2
D_EXPERT = 768
MOE_BLOCK = 128

kernel_name = "hybrid_mla_swa_hmoe_block"


def rms_norm(x, g):
    xf = x.astype(jnp.float32)
    y = xf * lax.rsqrt(jnp.mean(xf * xf, axis=-1, keepdims=True) + EPS)
    return (y * g.astype(jnp.float32)).astype(x.dtype)


def rope_tables(positions, dim):
    inv_freq = 1.0 / (ROPE_THETA ** (jnp.arange(0, dim, 2, dtype=jnp.float32) / dim))
    ang = positions.astype(jnp.float32)[..., None] * inv_freq
    return jnp.cos(ang)[:, :, None, :], jnp.sin(ang)[:, :, None, :]


def apply_rope(x, cos, sin):
    xf = x.astype(jnp.float32)
    x1, x2 = jnp.split(xf, 2, axis=-1)
    return jnp.concatenate([x1 * cos - x2 * sin, x2 * cos + x1 * sin], axis=-1).astype(x.dtype)


def split_columns(z):
    idx, run = [], 0
    for w in IN_WIDTHS[:-1]:
        run += w
        idx.append(run)
    return jnp.split(z, idx, axis=-1)


def mla_attention(c_q, c_kv, k_rope, g_qa, g_kva, w_uq, w_ukv, cos, sin):
    B, S, _ = c_q.shape
    q = (rms_norm(c_q, g_qa) @ w_uq).reshape(B, S, MLA_HEADS, MLA_QK_DIM)
    q = jnp.concatenate([q[..., :NOPE_DIM], apply_rope(q[..., NOPE_DIM:], cos, sin)], axis=-1)
    kv = (rms_norm(c_kv, g_kva) @ w_ukv).reshape(B, S, MLA_HEADS, NOPE_DIM + MLA_V_DIM)
    k_nope, v = kv[..., :NOPE_DIM], kv[..., NOPE_DIM:]
    k_pe = apply_rope(k_rope[:, :, None, :], cos, sin)
    k = jnp.concatenate([k_nope, jnp.broadcast_to(k_pe, (B, S, MLA_HEADS, ROPE_DIM))], axis=-1)
    nb = S // QBLOCK
    q_blocks = q.reshape(B, nb, QBLOCK, MLA_HEADS, MLA_QK_DIM).swapaxes(0, 1)
    scale = MLA_QK_DIM ** -0.5
    k_pos = jnp.arange(S)

    def block(args):
        qb, bi = args
        s = jnp.einsum('bqhd,bkhd->bhqk', qb, k, preferred_element_type=jnp.float32) * scale
        q_pos = bi * QBLOCK + jnp.arange(QBLOCK)
        s = jnp.where(k_pos[None, :] <= q_pos[:, None], s, NEG_INF)
        p = jax.nn.softmax(s, axis=-1).astype(v.dtype)
        return jnp.einsum('bhqk,bkhd->bqhd', p, v)

    o = lax.map(block, (q_blocks, jnp.arange(nb)))
    return o.swapaxes(0, 1).reshape(B, S, MLA_HEADS * MLA_V_DIM)


def swa_attention(q, k, v, sinks, cos, sin):
    B, S, _ = q.shape
    nb = S // WINDOW
    q = apply_rope(q.reshape(B, S, SWA_HEADS, SWA_HD), cos, sin)
    k = apply_rope(k.reshape(B, S, SWA_KV_HEADS, SWA_HD), cos, sin)
    v = v.reshape(B, S, SWA_KV_HEADS, SWA_HD)
    qb = q.reshape(B, nb, WINDOW, SWA_KV_HEADS, SWA_GROUP, SWA_HD)

    def with_prev(t):
        tb = t.reshape(B, nb, WINDOW, SWA_KV_HEADS, SWA_HD)
        prev = jnp.pad(tb, ((0, 0), (1, 0), (0, 0), (0, 0), (0, 0)))[:, :-1]
        return jnp.concatenate([prev, tb], axis=2)

    kb, vb = with_prev(k), with_prev(v)
    s = jnp.einsum('bnqkgd,bnckd->bnkgqc', qb, kb,
                   preferred_element_type=jnp.float32) * (SWA_HD ** -0.5)
    qi = WINDOW + jnp.arange(WINDOW)[:, None]
    kj = jnp.arange(2 * WINDOW)[None, :]
    band = (kj <= qi) & (qi - kj < WINDOW)
    valid = band[None] & ((jnp.arange(nb)[:, None, None] > 0) | (kj[None] >= WINDOW))
    s = jnp.where(valid[None, :, None, None], s, NEG_INF)
    sink = sinks.astype(jnp.float32).reshape(1, 1, SWA_KV_HEADS, SWA_GROUP, 1, 1)
    m = jnp.maximum(jnp.max(s, axis=-1, keepdims=True), sink)
    p = jnp.exp(s - m)
    p = (p / (jnp.sum(p, axis=-1, keepdims=True) + jnp.exp(sink - m))).astype(v.dtype)
    o = jnp.einsum('bnkgqc,bnckd->bnqkgd', p, vb)
    return o.reshape(B, S, SWA_HEADS * SWA_HD)


def memory_cross_attention(hn, mem_n, w_xq, w_xkv, w_xo):
    B, S, _ = hn.shape
    M = mem_n.shape[1]
    q = (hn @ w_xq).reshape(B, S, X_HEADS, X_HD)
    kv = (mem_n @ w_xkv).reshape(B, M, 2, X_HEADS, X_HD)
    k, v = kv[:, :, 0], kv[:, :, 1]
    s = jnp.einsum('bqhd,bmhd->bhqm', q, k, preferred_element_type=jnp.float32) * (X_HD ** -0.5)
    p = jax.nn.softmax(s, axis=-1).astype(v.dtype)
    o = jnp.einsum('bhqm,bmhd->bqhd', p, v)
    return o.reshape(B, S, X_HEADS * X_HD) @ w_xo


def hierarchical_moe(xn, w_group, b_group, w_router, b_router, w_gate, w_up, w_down):
    B, S, D = xn.shape
    T = B * S
    xf = xn.reshape(T, D)
    tok_ids = jnp.arange(T)
    g_logits = (xf @ w_group).astype(jnp.float32) + b_group.astype(jnp.float32)
    g_prob = jax.nn.softmax(g_logits, axis=-1)
    g_sel = jnp.argmax(g_logits, axis=-1)
    p_group = g_prob[tok_ids, g_sel][:, None]
    e_logits = ((xf @ w_router).astype(jnp.float32) + b_router.astype(jnp.float32))
    e_in = e_logits.reshape(T, N_GROUPS, EXPERTS_PER_GROUP)[tok_ids, g_sel]
    top_p, top_i = lax.top_k(jax.nn.softmax(e_in, axis=-1), TOP_K)
    gate = top_p / jnp.sum(top_p, axis=-1, keepdims=True) * p_group
    expert = g_sel[:, None] * EXPERTS_PER_GROUP + top_i
    A = T * TOP_K
    flat_e = expert.reshape(A)
    flat_t = jnp.repeat(tok_ids, TOP_K)
    flat_g = gate.reshape(A)
    order = jnp.argsort(flat_e)
    se, st, sg = flat_e[order], flat_t[order], flat_g[order]
    counts = jnp.zeros(N_EXPERTS, jnp.int32).at[flat_e].add(1)
    padded = (counts + MOE_BLOCK - 1) // MOE_BLOCK * MOE_BLOCK
    starts = jnp.cumsum(counts) - counts
    pad_ends = jnp.cumsum(padded)
    pad_starts = pad_ends - padded
    dest = pad_starts[se] + jnp.arange(A) - starts[se]
    n_blocks = -(-A // MOE_BLOCK) + N_EXPERTS
    P = n_blocks * MOE_BLOCK
    slot_tok = jnp.zeros(P, jnp.int32).at[dest].set(st)
    slot_gate = jnp.zeros(P, jnp.float32).at[dest].set(sg)
    blk_expert = jnp.minimum(
        jnp.searchsorted(pad_ends, jnp.arange(n_blocks) * MOE_BLOCK, side='right'), N_EXPERTS - 1)

    def run_block(args):
        e, tok, g = args
        xb = xf[tok]
        hb = jax.nn.silu(xb @ w_gate[e]) * (xb @ w_up[e])
        return (hb @ w_down[e]) * g[:, None].astype(xb.dtype)

    y = lax.map(run_block, (blk_expert, slot_tok.reshape(n_blocks, MOE_BLOCK),
                            slot_gate.reshape(n_blocks, MOE_BLOCK)))
    out = jnp.zeros((T, D), xn.dtype).at[slot_tok].add(y.reshape(P, D))
    return out.reshape(B, S, D)


def setup_inputs(seed: int = 0) -> dict:
    key = jax.random.key(seed)
    ks = iter(jax.random.split(key, 32))

    def dense(shape, fan_in):
        return jax.random.normal(next(ks), shape, jnp.float32) * (fan_in ** -0.5)

    def gain(shape):
        return 1.0 + 0.01 * jax.random.normal(next(ks), shape, jnp.float32)

    L = DEPTH
    x = jax.random.normal(next(ks), (BATCH, SEQ, D_MODEL), jnp.float32)
    mem = jax.random.normal(next(ks), (BATCH, MEM_LEN, D_MODEL), jnp.float32)
    offset = jax.random.randint(next(ks), (BATCH, 1), 0, 1024, jnp.int32)
    positions = offset + jnp.arange(SEQ, dtype=jnp.int32)[None, :]
    return {
        'x': x,
        'mem': mem,
        'positions': positions,
        'g_mix': gain((L, D_MODEL)),
        'w_in': dense((L, D_MODEL, D_IN), D_MODEL),
        'g_qa': gain((L, Q_LORA)),
        'g_kva': gain((L, KV_LORA)),
        'w_uq': dense((L, Q_LORA, MLA_HEADS * MLA_QK_DIM), Q_LORA),
        'w_ukv': dense((L, KV_LORA, MLA_HEADS * (NOPE_DIM + MLA_V_DIM)), KV_LORA),
        'sinks': 0.5 * jax.random.normal(next(ks), (L, SWA_HEADS), jnp.float32),
        'w_pa': dense((L, MLA_HEADS * MLA_V_DIM, D_MODEL), MLA_HEADS * MLA_V_DIM),
        'w_pb': dense((L, SWA_HEADS * SWA_HD, D_MODEL), SWA_HEADS * SWA_HD),
        'w_o': dense((L, D_MODEL, D_MODEL), D_MODEL),
        'g_cross': gain((L, D_MODEL)),
        'g_mem': gain((D_MODEL,)),
        'w_xq': dense((L, D_MODEL, X_HEADS * X_HD), D_MODEL),
        'w_xkv': dense((L, D_MODEL, 2 * X_HEADS * X_HD), D_MODEL),
        'w_xo': dense((L, X_HEADS * X_HD, D_MODEL), X_HEADS * X_HD),
        'g_ffn': gain((L, D_MODEL)),
        'w_group': dense((L, D_MODEL, N_GROUPS), D_MODEL),
        'b_group': 0.01 * jax.random.normal(next(ks), (L, N_GROUPS), jnp.float32),
        'w_router': dense((L, D_MODEL, N_EXPERTS), D_MODEL),
        'b_router': 0.01 * jax.random.normal(next(ks), (L, N_EXPERTS), jnp.float32),
        'w_gate': dense((L, N_EXPERTS, D_MODEL, D_EXPERT), D_MODEL),
        'w_up': dense((L, N_EXPERTS, D_MODEL, D_EXPERT), D_MODEL),
        'w_down': dense((L, N_EXPERTS, D_EXPERT, D_MODEL), D_EXPERT),
        'g_final': gain((D_MODEL,)),
    }


def reference(x, mem, positions, g_mix, w_in, g_qa, g_kva, w_uq, w_ukv, sinks, w_pa, w_pb, w_o,
              g_cross, g_mem, w_xq, w_xkv, w_xo, g_ffn, w_group, b_group, w_router, b_router,
              w_gate, w_up, w_down, g_final):
    cos, sin = rope_tables(positions, ROPE_DIM)
    mem_n = rms_norm(mem, g_mem)
    h = x
    for l in range(DEPTH):
        xn = rms_norm(h, g_mix[l])
        c_q, c_kv, k_rope, q_s, k_s, v_s, gate_a, gate_b = split_columns(xn @ w_in[l])
        o_a = mla_attention(c_q, c_kv, k_rope, g_qa[l], g_kva[l], w_uq[l], w_ukv[l], cos, sin)
        o_b = swa_attention(q_s, k_s, v_s, sinks[l], cos, sin)
        merged = (jax.nn.sigmoid(gate_a) * (o_a @ w_pa[l])
                  + jax.nn.sigmoid(gate_b) * (o_b @ w_pb[l]))
        h = h + merged @ w_o[l]
        h = h + memory_cross_attention(rms_norm(h, g_cross[l]), mem_n, w_xq[l], w_xkv[l], w_xo[l])
        h = h + hierarchical_moe(rms_norm(h, g_ffn[l]), w_group[l], b_group[l], w_router[l],
                                 b_router[l], w_gate[l], w_up[l], w_down[l])
    return rms_norm(h, g_final)
```

```python
import functools
from typing import NamedTuple

import jax
import jax.numpy as jnp
from jax import lax
from jax.experimental import pallas as pl
from jax.experimental.pallas import tpu as pltpu

F32 = jnp.float32
BF16 = jnp.bfloat16
EPS = 1e-6
ROPE_THETA = 10000.0
NEG_INF = -1e30
LANES = 128
ROPE_DIM = 64
NOPE_DIM = 128
MLA_V_DIM = 128
MLA_HEAD_PAD = 256
SWA_HD = 64
X_HD = 128
EPG = 8
TOP_K = 2
MIB = 1024 * 1024


class Dims(NamedTuple):
    batch: int = 4
    seq: int = 2048
    d_model: int = 4096
    mem_len: int = 256
    mla_heads: int = 16
    q_lora: int = 1024
    kv_lora: int = 512
    swa_heads: int = 32
    swa_kv_heads: int = 8
    window: int = 128
    x_heads: int = 4
    n_groups: int = 4
    d_expert: int = 768
    moe_block: int = 128
    tm: int = 1024
    tn_in: int = 512
    tn: int = 512
    tq: int = 256
    t_cross: int = 128
    t_tok: int = 256
    t_norm: int = 256

    @property
    def tokens(self):
        return self.batch * self.seq

    @property
    def n_experts(self):
        return self.n_groups * EPG

    @property
    def swa_q(self):
        return self.swa_heads * SWA_HD

    @property
    def swa_kv(self):
        return self.swa_kv_heads * SWA_HD

    @property
    def off_ckv(self):
        return self.q_lora

    @property
    def off_qs(self):
        return self.q_lora + self.kv_lora

    @property
    def off_ks(self):
        return self.off_qs + self.swa_q

    @property
    def off_vs(self):
        return self.off_ks + self.swa_kv

    @property
    def off_ga(self):
        return self.off_vs + self.swa_kv

    @property
    def off_gb(self):
        return self.off_ga + self.d_model

    @property
    def n_main(self):
        return self.off_gb + self.d_model

    @property
    def n_blocks(self):
        return -(-(self.tokens * TOP_K) // self.moe_block) + self.n_experts


def _exact_div(a, b):
    assert a % b == 0, (a, b)
    return a // b


def _cparams(sem, vmem_mib):
    return pltpu.CompilerParams(dimension_semantics=sem, vmem_limit_bytes=vmem_mib * MIB)


def _rms(x, g):
    return x * lax.rsqrt(jnp.mean(x * x, axis=-1, keepdims=True) + EPS) * g


def _rope128(x, cos, sin):
    lane = lax.broadcasted_iota(jnp.int32, x.shape, 1)
    first_half = (lane % ROPE_DIM) < (ROPE_DIM // 2)
    rot = jnp.where(first_half, -pltpu.roll(x, LANES - ROPE_DIM // 2, 1), pltpu.roll(x, ROPE_DIM // 2, 1))
    return x * cos + rot * sin


def _dot(a, b):
    return jnp.dot(a, b, preferred_element_type=F32)


def _dot_nt(a, b):
    return lax.dot_general(a, b, (((1,), (1,)), ((), ())), preferred_element_type=F32)


def _rope_table_kernel(pos_ref, inv_ref, cos_ref, sin_ref):
    ang = pos_ref[...].astype(F32) * inv_ref[...]
    cos_ref[...] = jnp.cos(ang)
    sin_ref[...] = jnp.sin(ang)


def _rope_tables(positions, d):
    t = d.tokens
    half = ROPE_DIM // 2
    inv_freq = 1.0 / (ROPE_THETA ** (jnp.arange(0, ROPE_DIM, 2, dtype=F32) / ROPE_DIM))
    inv = jnp.tile(inv_freq, LANES // half).reshape(1, LANES)
    tb = min(t, 1024)
    return pl.pallas_call(
        _rope_table_kernel,
        out_shape=(jax.ShapeDtypeStruct((t, LANES), F32),) * 2,
        grid=(_exact_div(t, tb),),
        in_specs=[pl.BlockSpec((tb, 1), lambda i: (i, 0)), pl.BlockSpec((1, LANES), lambda i: (0, 0))],
        out_specs=(pl.BlockSpec((tb, LANES), lambda i: (i, 0)),) * 2,
        name="rope_tables",
    )(positions.reshape(t, 1), inv)


def _rmsnorm_kernel(x_ref, g_ref, o_ref):
    o_ref[...] = _rms(x_ref[...].astype(F32), g_ref[...]).astype(o_ref.dtype)


def _rmsnorm(x, g, out_dtype, tm):
    m, dd = x.shape
    return pl.pallas_call(
        _rmsnorm_kernel,
        out_shape=jax.ShapeDtypeStruct((m, dd), out_dtype),
        grid=(_exact_div(m, tm),),
        in_specs=[pl.BlockSpec((tm, dd), lambda i: (i, 0)), pl.BlockSpec((1, dd), lambda i: (0, 0))],
        out_specs=pl.BlockSpec((tm, dd), lambda i: (i, 0)),
        compiler_params=_cparams(("parallel",), 32),
        name="rmsnorm",
    )(x, g.reshape(1, dd).astype(F32))


def _mm_kernel(*refs, pre, post, n_extra):
    a_ref, w_ref = refs[0], refs[1]
    extras = refs[2:2 + n_extra]
    out_ref = refs[2 + n_extra]
    a = a_ref[...]
    if pre is not None:
        a = pre(a, extras)
    post(_dot(a, w_ref[...]), extras, out_ref)


def _matmul(a, w, *, k, a_col, tm, tn, out_dtype, post, pre=None, extras=(), vmem_mib=48, name):
    m = a.shape[0]
    n = w.shape[1]
    assert w.shape[0] == k
    in_specs = [pl.BlockSpec((tm, k), lambda i, j: (i, a_col)), pl.BlockSpec((k, tn), lambda i, j: (0, j))]
    in_specs += [s for _, s in extras]
    return pl.pallas_call(
        functools.partial(_mm_kernel, pre=pre, post=post, n_extra=len(extras)),
        out_shape=jax.ShapeDtypeStruct((m, n), out_dtype),
        grid=(_exact_div(m, tm), _exact_div(n, tn)),
        in_specs=in_specs,
        out_specs=pl.BlockSpec((tm, tn), lambda i, j: (i, j)),
        compiler_params=_cparams(("parallel", "arbitrary"), vmem_mib),
        name=name,
    )(a, w, *[x for x, _ in extras])


def _post_cast(acc, extras, o_ref):
    o_ref[...] = acc.astype(o_ref.dtype)


def _pre_rms(a, extras):
    return _rms(a.astype(F32), extras[0][...]).astype(BF16)


def _post_residual(acc, extras, o_ref):
    o_ref[...] = extras[0][...] + acc


def _post_in_proj(acc, extras, o_ref, *, j_rope0, j_k, j_rope1, tn, q_scale):
    cos_ref, sin_ref = extras
    j = pl.program_id(1)
    is_rope = (j >= j_rope0) & (j < j_rope1)

    @pl.when(is_rope)
    def _():
        scale = jnp.where(j < j_k, q_scale, 1.0).astype(F32)
        cos = cos_ref[...]
        sin = sin_ref[...]
        for c in range(tn // LANES):
            sl = slice(c * LANES, (c + 1) * LANES)
            o_ref[:, sl] = (_rope128(acc[:, sl], cos, sin) * scale).astype(o_ref.dtype)

    @pl.when(jnp.logical_not(is_rope))
    def _():
        o_ref[...] = acc.astype(o_ref.dtype)


def _post_rope_all(acc, extras, o_ref):
    cos_ref, sin_ref = extras
    cos = cos_ref[...]
    sin = sin_ref[...]
    for c in range(acc.shape[1] // LANES):
        sl = slice(c * LANES, (c + 1) * LANES)
        o_ref[:, sl] = _rope128(acc[:, sl], cos, sin).astype(o_ref.dtype)


def _post_mla_q(acc, extras, o_ref, *, scale):
    _, cos_ref, sin_ref = extras
    cos = cos_ref[...]
    sin = sin_ref[...]
    for hd in range(acc.shape[1] // MLA_HEAD_PAD):
        lo = slice(hd * MLA_HEAD_PAD, hd * MLA_HEAD_PAD + LANES)
        hi = slice(hd * MLA_HEAD_PAD + LANES, (hd + 1) * MLA_HEAD_PAD)
        o_ref[:, lo] = (acc[:, lo] * scale).astype(o_ref.dtype)
        o_ref[:, hi] = (_rope128(acc[:, hi], cos, sin) * scale).astype(o_ref.dtype)


def _merge_kernel(oa_ref, wpa_ref, ob_ref, wpb_ref, ga_ref, gb_ref, o_ref):
    pa = _dot(oa_ref[...], wpa_ref[...])
    pb = _dot(ob_ref[...], wpb_ref[...])
    sa = 1.0 / (1.0 + jnp.exp(-ga_ref[...].astype(F32)))
    sb = 1.0 / (1.0 + jnp.exp(-gb_ref[...].astype(F32)))
    o_ref[...] = (sa * pa + sb * pb).astype(o_ref.dtype)


def _merge(o_a, w_pa, o_b, w_pb, z, d):
    t = d.tokens
    tm, tn = d.tm, d.tn
    ka, kb = o_a.shape[1], o_b.shape[1]
    ja, jb = _exact_div(d.off_ga, tn), _exact_div(d.off_gb, tn)
    return pl.pallas_call(
        _merge_kernel,
        out_shape=jax.ShapeDtypeStruct((t, d.d_model), BF16),
        grid=(_exact_div(t, tm), _exact_div(d.d_model, tn)),
        in_specs=[
            pl.BlockSpec((tm, ka), lambda i, j: (i, 0)),
            pl.BlockSpec((ka, tn), lambda i, j: (0, j)),
            pl.BlockSpec((tm, kb), lambda i, j: (i, 0)),
            pl.BlockSpec((kb, tn), lambda i, j: (0, j)),
            pl.BlockSpec((tm, tn), lambda i, j: (i, j + ja)),
            pl.BlockSpec((tm, tn), lambda i, j: (i, j + jb)),
        ],
        out_specs=pl.BlockSpec((tm, tn), lambda i, j: (i, j)),
        compiler_params=_cparams(("parallel", "arbitrary"), 48),
        name="gated_merge",
    )(o_a, w_pa, o_b, w_pb, z, z)


def _mla_kernel(q_ref, kv_ref, kpe_ref, o_ref, k_scr, *, seq, tq):
    k_scr[:, :NOPE_DIM] = kv_ref[:, :NOPE_DIM]
    k_scr[:, NOPE_DIM:] = kpe_ref[...]
    for i in range(seq // tq):
        ln = (i + 1) * tq
        q = q_ref[i * tq:(i + 1) * tq, :]
        s = _dot_nt(q, k_scr[:ln, :])
        row = lax.broadcasted_iota(jnp.int32, (tq, ln), 0) + i * tq
        col = lax.broadcasted_iota(jnp.int32, (tq, ln), 1)
        s = jnp.where(col <= row, s, NEG_INF)
        m = jnp.max(s, axis=-1, keepdims=True)
        p = jnp.exp(s - m)
        l = jnp.sum(p, axis=-1, keepdims=True)
        o = _dot(p.astype(BF16), kv_ref[:ln, NOPE_DIM:])
        o_ref[i * tq:(i + 1) * tq, :] = (o / l).astype(o_ref.dtype)


def _mla_attention(q_full, kv, kpe, d):
    t = d.tokens
    return pl.pallas_call(
        functools.partial(_mla_kernel, seq=d.seq, tq=d.tq),
        out_shape=jax.ShapeDtypeStruct((t, d.mla_heads * MLA_V_DIM), BF16),
        grid=(d.batch, d.mla_heads),
        in_specs=[
            pl.BlockSpec((d.seq, MLA_HEAD_PAD), lambda b, h: (b, h)),
            pl.BlockSpec((d.seq, NOPE_DIM + MLA_V_DIM), lambda b, h: (b, h)),
            pl.BlockSpec((d.seq, LANES), lambda b, h: (b, 0)),
        ],
        out_specs=pl.BlockSpec((d.seq, MLA_V_DIM), lambda b, h: (b, h)),
        scratch_shapes=[pltpu.VMEM((d.seq, MLA_HEAD_PAD), BF16)],
        compiler_params=_cparams(("parallel", "parallel"), 48),
        name="mla_attention",
    )(q_full, kv, kpe)


def _swa_kernel(sink_ref, q_ref, k_ref, v_ref, o_ref, klo, khi, vlo, vhi, *, seq, window, heads_per_step):
    pair = pl.program_id(1)
    w = window
    lane = lax.broadcasted_iota(jnp.int32, (seq, LANES), 1)
    low = lane < SWA_HD
    for src_ref, lo_ref, hi_ref in ((k_ref, klo, khi), (v_ref, vlo, vhi)):
        x = src_ref[...].astype(F32)
        xs = pltpu.roll(x, SWA_HD, 1)
        zero = jnp.zeros_like(x)
        lo_ref[0] = jnp.where(low, x, zero).astype(BF16)
        hi_ref[0] = jnp.where(low, zero, xs).astype(BF16)
        lo_ref[1] = jnp.where(low, xs, zero).astype(BF16)
        hi_ref[1] = jnp.where(low, zero, x).astype(BF16)

    n_tiles = heads_per_step * SWA_HD // LANES
    tiles_per_kv = n_tiles // 2

    def block(r0, k0, klen, is_first):
        qq = lax.broadcasted_iota(jnp.int32, (w, klen), 0)
        kk = lax.broadcasted_iota(jnp.int32, (w, klen), 1)
        mask = (kk <= qq) if is_first else ((kk > qq) & (kk <= qq + w))
        for c in range(n_tiles):
            g = c // tiles_per_kv
            q = q_ref[pl.ds(r0, w), c * LANES:(c + 1) * LANES]
            acc = None
            for half, (kref, vref) in enumerate(((klo, vlo), (khi, vhi))):
                sink = sink_ref[pair * heads_per_step + 2 * c + half]
                s = _dot_nt(q, kref[g, pl.ds(k0, klen), :])
                s = jnp.where(mask, s, NEG_INF)
                m = jnp.maximum(jnp.max(s, axis=-1, keepdims=True), sink)
                p = jnp.exp(s - m)
                den = jnp.sum(p, axis=-1, keepdims=True) + jnp.exp(sink - m)
                pv = _dot((p / den).astype(BF16), vref[g, pl.ds(k0, klen), :])
                acc = pv if acc is None else acc + pv
            o_ref[pl.ds(r0, w), c * LANES:(c + 1) * LANES] = acc.astype(o_ref.dtype)

    block(0, 0, w, True)

    def body(n, carry):
        r0 = pl.multiple_of(n * w, w)
        block(r0, pl.multiple_of(r0 - w, w), 2 * w, False)
        return carry

    lax.fori_loop(1, seq // w, body, 0)


def _swa_attention(z, sinks, d):
    t = d.tokens
    hps = 2 * (d.swa_heads // d.swa_kv_heads)
    qw = hps * SWA_HD
    n_pairs = _exact_div(d.swa_kv_heads, 2)
    jq, jk, jv = _exact_div(d.off_qs, qw), _exact_div(d.off_ks, LANES), _exact_div(d.off_vs, LANES)
    grid_spec = pltpu.PrefetchScalarGridSpec(
        num_scalar_prefetch=1,
        grid=(d.batch, n_pairs),
        in_specs=[
            pl.BlockSpec((d.seq, qw), lambda b, p, s: (b, jq + p)),
            pl.BlockSpec((d.seq, LANES), lambda b, p, s: (b, jk + p)),
            pl.BlockSpec((d.seq, LANES), lambda b, p, s: (b, jv + p)),
        ],
        out_specs=pl.BlockSpec((d.seq, qw), lambda b, p, s: (b, p)),
        scratch_shapes=[pltpu.VMEM((2, d.seq, LANES), BF16)] * 4,
    )
    return pl.pallas_call(
        functools.partial(_swa_kernel, seq=d.seq, window=d.window, heads_per_step=hps),
        out_shape=jax.ShapeDtypeStruct((t, d.swa_q), BF16),
        grid_spec=grid_spec,
        compiler_params=_cparams(("parallel", "parallel"), 48),
        name="swa_attention",
    )(sinks.astype(F32), z, z, z)


def _cross_router_kernel(h_ref, gc_ref, wq_ref, kvm_ref, wo_ref, gf_ref, wr_ref, br_ref,
                         h2_ref, xnf_ref, rt_ref, cnt_ref, run_ref, *, x_heads, n_groups, scale):
    tm = h_ref.shape[0]
    hx = x_heads * X_HD
    n_exp = n_groups * EPG

    @pl.when(pl.program_id(0) == 0)
    def _():
        run_ref[...] = jnp.zeros_like(run_ref)

    h = h_ref[...]
    hn = _rms(h, gc_ref[...]).astype(BF16)
    q = (_dot(hn, wq_ref[...]) * scale).astype(BF16)
    outs = []
    for hd in range(x_heads):
        kh = kvm_ref[:, hd * X_HD:(hd + 1) * X_HD]
        vh = kvm_ref[:, hx + hd * X_HD:hx + (hd + 1) * X_HD]
        s = _dot_nt(q[:, hd * X_HD:(hd + 1) * X_HD], kh)
        m = jnp.max(s, axis=-1, keepdims=True)
        p = jnp.exp(s - m)
        l = jnp.sum(p, axis=-1, keepdims=True)
        outs.append((_dot(p.astype(BF16), vh) / l).astype(BF16))
    h2 = h + _dot(jnp.concatenate(outs, axis=1), wo_ref[...])
    h2_ref[...] = h2
    xnf = _rms(h2, gf_ref[...])
    xnf_ref[...] = xnf

    logits = _dot(xnf.astype(BF16), wr_ref[...]) + br_ref[...]
    lane = lax.broadcasted_iota(jnp.int32, (tm, LANES), 1).astype(F32)
    big = float(LANES)
    is_group = (lane >= n_exp) & (lane < n_exp + n_groups)
    gl = jnp.where(is_group, logits, -jnp.inf)
    gmax = jnp.max(gl, axis=-1, keepdims=True)
    g_lane = jnp.min(jnp.where(gl == gmax, lane, big), axis=-1, keepdims=True)
    p_group = 1.0 / jnp.sum(jnp.where(is_group, jnp.exp(gl - gmax), 0.0), axis=-1, keepdims=True)
    e_lo = (g_lane - n_exp) * EPG
    in_group = (lane >= e_lo) & (lane < e_lo + EPG)
    el = jnp.where(in_group, logits, -jnp.inf)
    m1 = jnp.max(el, axis=-1, keepdims=True)
    i1 = jnp.min(jnp.where(el == m1, lane, big), axis=-1, keepdims=True)
    el2 = jnp.where(lane == i1, -jnp.inf, el)
    m2 = jnp.max(el2, axis=-1, keepdims=True)
    i2 = jnp.min(jnp.where(el2 == m2, lane, big), axis=-1, keepdims=True)
    w2 = jnp.exp(m2 - m1)
    gate1 = p_group / (1.0 + w2)
    gate2 = gate1 * w2

    hot1 = lane == i1
    hot2 = lane == i2
    onehot = jnp.where(hot1 | hot2, 1.0, 0.0)
    rr = lax.broadcasted_iota(jnp.int32, (tm, tm), 0)
    cc = lax.broadcasted_iota(jnp.int32, (tm, tm), 1)
    tri = jnp.where(cc < rr, 1.0, 0.0).astype(BF16)
    before = _dot(tri, onehot.astype(BF16)) + run_ref[...]
    rank1 = jnp.sum(jnp.where(hot1, before, 0.0), axis=-1, keepdims=True)
    rank2 = jnp.sum(jnp.where(hot2, before, 0.0), axis=-1, keepdims=True)
    run = run_ref[...] + jnp.sum(onehot, axis=0, keepdims=True)
    run_ref[...] = run
    cnt_ref[...] = run

    rt = jnp.where(lane == 0, i1, 0.0)
    rt = jnp.where(lane == 1, i2, rt)
    rt = jnp.where(lane == 2, gate1, rt)
    rt = jnp.where(lane == 3, gate2, rt)
    rt = jnp.where(lane == 4, rank1, rt)
    rt = jnp.where(lane == 5, rank2, rt)
    rt_ref[...] = rt


def _cross_router(h, g_cross, w_xq, kvm, w_xo, g_ffn, w_r, b_r, d):
    t = d.tokens
    tm = d.t_cross
    dm = d.d_model
    hx = d.x_heads * X_HD
    steps_per_batch = _exact_div(d.seq, tm)
    row = lambda i: (i, 0)
    fixed = lambda i: (0, 0)
    return pl.pallas_call(
        functools.partial(_cross_router_kernel, x_heads=d.x_heads, n_groups=d.n_groups, scale=X_HD ** -0.5),
        out_shape=(
            jax.ShapeDtypeStruct((t, dm), F32),
            jax.ShapeDtypeStruct((t, dm), F32),
            jax.ShapeDtypeStruct((t, LANES), F32),
            jax.ShapeDtypeStruct((1, LANES), F32),
        ),
        grid=(_exact_div(t, tm),),
        in_specs=[
            pl.BlockSpec((tm, dm), row),
            pl.BlockSpec((1, dm), fixed),
            pl.BlockSpec((dm, hx), fixed),
            pl.BlockSpec((d.mem_len, 2 * hx), lambda i: (i // steps_per_batch, 0)),
            pl.BlockSpec((hx, dm), fixed),
            pl.BlockSpec((1, dm), fixed),
            pl.BlockSpec((dm, LANES), fixed),
            pl.BlockSpec((1, LANES), fixed),
        ],
        out_specs=(
            pl.BlockSpec((tm, dm), row),
            pl.BlockSpec((tm, dm), row),
            pl.BlockSpec((tm, LANES), row),
            pl.BlockSpec((1, LANES), fixed),
        ),
        scratch_shapes=[pltpu.VMEM((1, LANES), F32)],
        compiler_params=_cparams(("arbitrary",), 56),
        name="cross_attention_router",
    )(h, g_cross.reshape(1, dm), w_xq, kvm, w_xo, g_ffn.reshape(1, dm), w_r, b_r)


def _row_copy(src_hbm, src_row, dst_ref, dst_row, sem):
    return pltpu.make_async_copy(src_hbm.at[pl.ds(src_row, 1)], dst_ref.at[pl.ds(dst_row, 1)], sem)


def _dispatch_kernel(dest_ref, x_hbm, init_hbm, xs_hbm, sem, *, t_tok):
    del init_hbm
    base = pl.program_id(0) * t_tok

    def issue(r, carry):
        for k in range(TOP_K):
            _row_copy(x_hbm, base + r, xs_hbm, dest_ref[TOP_K * (base + r) + k], sem).start()
        return carry

    def drain(r, carry):
        for k in range(TOP_K):
            _row_copy(x_hbm, 0, xs_hbm, 0, sem).wait()
        return carry

    lax.fori_loop(0, t_tok, issue, 0)
    lax.fori_loop(0, t_tok, drain, 0)


def _dispatch(dest, x, d):
    t = d.tokens
    p_rows = d.n_blocks * d.moe_block
    init = jnp.zeros((p_rows, d.d_model), x.dtype)
    grid_spec = pltpu.PrefetchScalarGridSpec(
        num_scalar_prefetch=1,
        grid=(_exact_div(t, d.t_tok),),
        in_specs=[pl.BlockSpec(memory_space=pl.ANY), pl.BlockSpec(memory_space=pl.ANY)],
        out_specs=pl.BlockSpec(memory_space=pl.ANY),
        scratch_shapes=[pltpu.SemaphoreType.DMA(())],
    )
    return pl.pallas_call(
        functools.partial(_dispatch_kernel, t_tok=d.t_tok),
        out_shape=jax.ShapeDtypeStruct((p_rows, d.d_model), x.dtype),
        grid_spec=grid_spec,
        input_output_aliases={2: 0},
        compiler_params=pltpu.CompilerParams(dimension_semantics=("arbitrary",), has_side_effects=True),
        name="moe_dispatch",
    )(dest, x, init)


def _expert_kernel(be_ref, nu_ref, x_ref, wg_ref, wu_ref, wd_ref, y_ref):
    del be_ref
    used = pl.program_id(0) < nu_ref[0]

    @pl.when(jnp.logical_not(used))
    def _():
        y_ref[...] = jnp.zeros_like(y_ref)

    @pl.when(used)
    def _():
        x = x_ref[...].astype(BF16)
        a = _dot(x, wg_ref[0])
        b = _dot(x, wu_ref[0])
        hb = (a / (1.0 + jnp.exp(-a))) * b
        y_ref[...] = _dot(hb.astype(BF16), wd_ref[0])


def _experts(blk_expert, n_used, xs, w_gate, w_up, w_down, d):
    bm = d.moe_block
    dm, de = d.d_model, d.d_expert
    last = lambda i, be, nu: (jnp.minimum(i, nu[0] - 1), 0)
    grid_spec = pltpu.PrefetchScalarGridSpec(
        num_scalar_prefetch=2,
        grid=(d.n_blocks,),
        in_specs=[
            pl.BlockSpec((bm, dm), last),
            pl.BlockSpec((1, dm, de), lambda i, be, nu: (be[i], 0, 0)),
            pl.BlockSpec((1, dm, de), lambda i, be, nu: (be[i], 0, 0)),
            pl.BlockSpec((1, de, dm), lambda i, be, nu: (be[i], 0, 0)),
        ],
        out_specs=pl.BlockSpec((bm, dm), lambda i, be, nu: (i, 0)),
    )
    return pl.pallas_call(
        _expert_kernel,
        out_shape=jax.ShapeDtypeStruct((d.n_blocks * bm, dm), F32),
        grid_spec=grid_spec,
        compiler_params=_cparams(("arbitrary",), 56),
        name="moe_experts",
    )(blk_expert, n_used, xs, w_gate, w_up, w_down)


def _combine_kernel(dest_ref, y_hbm, h_ref, rt_ref, g_ref, *rest, t_tok, last):
    if last:
        out_ref, ybuf, sem = rest
    else:
        h3_ref, out_ref, ybuf, sem = rest
    base = pl.program_id(0) * t_tok

    def issue(r, carry):
        for k in range(TOP_K):
            _row_copy(y_hbm, dest_ref[TOP_K * (base + r) + k], ybuf.at[k], r, sem).start()
        return carry

    def drain(r, carry):
        for k in range(TOP_K):
            _row_copy(y_hbm, 0, ybuf.at[k], 0, sem).wait()
        return carry

    lax.fori_loop(0, t_tok, issue, 0)
    lax.fori_loop(0, t_tok, drain, 0)
    rt = rt_ref[...]
    h3 = h_ref[...] + rt[:, 2:3] * ybuf[0] + rt[:, 3:4] * ybuf[1]
    if not last:
        h3_ref[...] = h3
    out_ref[...] = _rms(h3, g_ref[...]).astype(out_ref.dtype)


def _combine(dest, y, h2, rt, g_next, d, last):
    t = d.tokens
    tt = d.t_tok
    dm = d.d_model
    row = lambda i, ds: (i, 0)
    out_specs = pl.BlockSpec((tt, dm), row)
    if last:
        out_shape = jax.ShapeDtypeStruct((t, dm), F32)
    else:
        out_shape = (jax.ShapeDtypeStruct((t, dm), F32), jax.ShapeDtypeStruct((t, dm), BF16))
        out_specs = (out_specs, pl.BlockSpec((tt, dm), row))
    grid_spec = pltpu.PrefetchScalarGridSpec(
        num_scalar_prefetch=1,
        grid=(_exact_div(t, tt),),
        in_specs=[
            pl.BlockSpec(memory_space=pl.ANY),
            pl.BlockSpec((tt, dm), row),
            pl.BlockSpec((tt, LANES), row),
            pl.BlockSpec((1, dm), lambda i, ds: (0, 0)),
        ],
        out_specs=out_specs,
        scratch_shapes=[pltpu.VMEM((TOP_K, tt, dm), F32), pltpu.SemaphoreType.DMA(())],
    )
    return pl.pallas_call(
        functools.partial(_combine_kernel, t_tok=tt, last=last),
        out_shape=out_shape,
        grid_spec=grid_spec,
        compiler_params=_cparams(("arbitrary",), 48),
        name="moe_combine",
    )(dest, y, h2, rt, g_next.reshape(1, dm))


def _layer(h, xn, cos, sin, memn, p, d, g_next, last):
    t = d.tokens
    dm = d.d_model
    tm = d.tm
    row128 = pl.BlockSpec((tm, LANES), lambda i, j: (i, 0))

    tn = d.tn_in
    z = _matmul(
        xn, p["w_main"], k=dm, a_col=0, tm=tm, tn=tn, out_dtype=BF16,
        post=functools.partial(_post_in_proj, j_rope0=_exact_div(d.off_qs, tn), j_k=_exact_div(d.off_ks, tn),
                               j_rope1=_exact_div(d.off_vs, tn), tn=tn, q_scale=SWA_HD ** -0.5),
        extras=((cos, row128), (sin, row128)), name="in_proj")
    kpe = _matmul(xn, p["w_kr"], k=dm, a_col=0, tm=tm, tn=LANES, out_dtype=BF16, post=_post_rope_all,
                  extras=((cos, row128), (sin, row128)), name="rope_key_proj")

    tn_q = 2 * MLA_HEAD_PAD
    q_full = _matmul(
        z, p["w_uq"], k=d.q_lora, a_col=0, tm=tm, tn=tn_q, out_dtype=BF16, pre=_pre_rms,
        post=functools.partial(_post_mla_q, scale=(NOPE_DIM + ROPE_DIM) ** -0.5),
        extras=((p["g_qa"], pl.BlockSpec((1, d.q_lora), lambda i, j: (0, 0))), (cos, row128), (sin, row128)),
        name="mla_q_proj")
    kv = _matmul(
        z, p["w_ukv"], k=d.kv_lora, a_col=_exact_div(d.off_ckv, d.kv_lora), tm=tm, tn=d.tn, out_dtype=BF16,
        pre=_pre_rms, post=_post_cast,
        extras=((p["g_kva"], pl.BlockSpec((1, d.kv_lora), lambda i, j: (0, 0))),), name="mla_kv_proj")
    o_a = _mla_attention(q_full, kv, kpe, d)

    o_b = _swa_attention(z, p["sinks"], d)

    merged = _merge(o_a, p["w_pa"], o_b, p["w_pb"], z, d)
    h1 = _matmul(merged, p["w_o"], k=dm, a_col=0, tm=tm, tn=d.tn, out_dtype=F32, post=_post_residual,
                 extras=((h, pl.BlockSpec((tm, d.tn), lambda i, j: (i, j))),), name="out_proj")

    kvm = _matmul(memn, p["w_xkv"], k=dm, a_col=0, tm=memn.shape[0], tn=d.tn, out_dtype=BF16,
                  post=_post_cast, name="mem_kv_proj")
    h2, xnf, rt, cnt = _cross_router(h1, p["g_cross"], p["w_xq"], kvm, p["w_xo"], p["g_ffn"],
                                     p["w_r"], p["b_r"], d)

    bm = d.moe_block
    counts = cnt[0, :d.n_experts].astype(jnp.int32)
    padded = (counts + bm - 1) // bm * bm
    pad_ends = jnp.cumsum(padded)
    pad_starts = pad_ends - padded
    expert = rt[:, 0:TOP_K].astype(jnp.int32)
    dest = (pad_starts[expert] + rt[:, 4:4 + TOP_K].astype(jnp.int32)).reshape(t * TOP_K)
    blk_expert = jnp.minimum(
        jnp.searchsorted(pad_ends, jnp.arange(d.n_blocks, dtype=jnp.int32) * bm, side="right"),
        d.n_experts - 1).astype(jnp.int32)
    n_used = (pad_ends[-1:] // bm).astype(jnp.int32)

    xs = _dispatch(dest, xnf, d)
    y = _experts(blk_expert, n_used, xs, p["w_gate"], p["w_up"], p["w_down"], d)
    return _combine(dest, y, h2, rt, g_next, d, last)


def _prep_layer(l, d, g_mix, w_in, g_qa, g_kva, w_uq, w_ukv, sinks, w_pa, w_pb, w_o, g_cross, w_xq, w_xkv,
                w_xo, g_ffn, w_group, b_group, w_router, b_router, w_gate, w_up, w_down):
    dm = d.d_model
    kr0 = d.q_lora + d.kv_lora
    w = w_in[l]
    w_main = jnp.concatenate([w[:, :kr0], w[:, kr0 + ROPE_DIM:]], axis=1).astype(BF16)
    w_kr = jnp.pad(w[:, kr0:kr0 + ROPE_DIM], ((0, 0), (0, LANES - ROPE_DIM))).astype(BF16)
    qk = NOPE_DIM + ROPE_DIM
    wq = w_uq[l].reshape(d.q_lora, d.mla_heads, qk)
    wq = jnp.pad(wq, ((0, 0), (0, 0), (0, MLA_HEAD_PAD - qk))).reshape(d.q_lora, d.mla_heads * MLA_HEAD_PAD)
    n_r = d.n_experts + d.n_groups
    w_r = jnp.pad(jnp.concatenate([w_router[l], w_group[l]], axis=1), ((0, 0), (0, LANES - n_r)))
    b_r = jnp.pad(jnp.concatenate([b_router[l], b_group[l]]), (0, LANES - n_r)).reshape(1, LANES)
    return dict(
        w_main=w_main, w_kr=w_kr, w_uq=wq.astype(BF16), w_ukv=w_ukv[l].astype(BF16),
        g_qa=g_qa[l].reshape(1, -1).astype(F32), g_kva=g_kva[l].reshape(1, -1).astype(F32),
        sinks=sinks[l], w_pa=w_pa[l].astype(BF16), w_pb=w_pb[l].astype(BF16), w_o=w_o[l].astype(BF16),
        g_cross=g_cross[l].astype(F32), w_xq=w_xq[l].astype(BF16), w_xkv=w_xkv[l].astype(BF16),
        w_xo=w_xo[l].astype(BF16), g_ffn=g_ffn[l].astype(F32), w_r=w_r.astype(BF16), b_r=b_r.astype(F32),
        w_gate=w_gate[l].astype(BF16), w_up=w_up[l].astype(BF16), w_down=w_down[l].astype(BF16),
    )


def _forward(d, x, mem, positions, g_mix, w_in, g_qa, g_kva, w_uq, w_ukv, sinks, w_pa, w_pb, w_o,
             g_cross, g_mem, w_xq, w_xkv, w_xo, g_ffn, w_group, b_group, w_router, b_router,
             w_gate, w_up, w_down, g_final):
    depth = w_in.shape[0]
    t = d.tokens
    dm = d.d_model
    cos, sin = _rope_tables(positions, d)
    memn = _rmsnorm(mem.reshape(d.batch * d.mem_len, dm), g_mem, BF16, d.t_norm)
    h = x.reshape(t, dm)
    xn = _rmsnorm(h, g_mix[0], BF16, d.t_norm)
    for l in range(depth):
        p = _prep_layer(l, d, g_mix, w_in, g_qa, g_kva, w_uq, w_ukv, sinks, w_pa, w_pb, w_o, g_cross,
                        w_xq, w_xkv, w_xo, g_ffn, w_group, b_group, w_router, b_router, w_gate, w_up, w_down)
        last = l == depth - 1
        g_next = g_final if last else g_mix[l + 1]
        res = _layer(h, xn, cos, sin, memn, p, d, g_next, last)
        if last:
            return res.reshape(d.batch, d.seq, dm)
        h, xn = res


def kernel(x, mem, positions, g_mix, w_in, g_qa, g_kva, w_uq, w_ukv, sinks, w_pa, w_pb, w_o, g_cross, g_mem,
           w_xq, w_xkv, w_xo, g_ffn, w_group, b_group, w_router, b_router, w_gate, w_up, w_down, g_final):
    return _forward(Dims(), x, mem, positions, g_mix, w_in, g_qa, g_kva, w_uq, w_ukv, sinks, w_pa, w_pb, w_o,
                    g_cross, g_mem, w_xq, w_xkv, w_xo, g_ffn, w_group, b_group, w_router, b_router,
                    w_gate, w_up, w_down, g_final)
```

```python
import functools
from typing import NamedTuple

import jax
import jax.numpy as jnp
from jax import lax
from jax.experimental import pallas as pl
from jax.experimental.pallas import tpu as pltpu

F32 = jnp.float32
BF16 = jnp.bfloat16
EPS = 1e-6
ROPE_THETA = 10000.0
NEG_INF = -1e30
LANES = 128
ROPE_DIM = 64
NOPE_DIM = 128
MLA_V_DIM = 128
MLA_HEAD_PAD = 256
SWA_HD = 64
X_HD = 128
EPG = 8
TOP_K = 2
MIB = 1024 * 1024


class Dims(NamedTuple):
    batch: int = 4
    seq: int = 2048
    d_model: int = 4096
    mem_len: int = 256
    mla_heads: int = 16
    q_lora: int = 1024
    kv_lora: int = 512
    swa_heads: int = 32
    swa_kv_heads: int = 8
    window: int = 128
    x_heads: int = 4
    n_groups: int = 4
    d_expert: int = 768
    moe_block: int = 128
    tm: int = 1024
    tn_in: int = 512
    tn: int = 512
    tq: int = 256
    t_cross: int = 128
    t_tok: int = 256
    t_norm: int = 256

    @property
    def tokens(self):
        return self.batch * self.seq

    @property
    def n_experts(self):
        return self.n_groups * EPG

    @property
    def swa_q(self):
        return self.swa_heads * SWA_HD

    @property
    def swa_kv(self):
        return self.swa_kv_heads * SWA_HD

    @property
    def off_ckv(self):
        return self.q_lora

    @property
    def off_qs(self):
        return self.q_lora + self.kv_lora

    @property
    def off_ks(self):
        return self.off_qs + self.swa_q

    @property
    def off_vs(self):
        return self.off_ks + self.swa_kv

    @property
    def off_ga(self):
        return self.off_vs + self.swa_kv

    @property
    def off_gb(self):
        return self.off_ga + self.d_model

    @property
    def n_main(self):
        return self.off_gb + self.d_model

    @property
    def n_blocks(self):
        return -(-(self.tokens * TOP_K) // self.moe_block) + self.n_experts


def _exact_div(a, b):
    assert a % b == 0, (a, b)
    return a // b


def _cparams(sem, vmem_mib):
    return pltpu.CompilerParams(dimension_semantics=sem, vmem_limit_bytes=vmem_mib * MIB)


def _rms(x, g):
    return x * lax.rsqrt(jnp.mean(x * x, axis=-1, keepdims=True) + EPS) * g


def _rope128(x, cos, sin):
    lane = lax.broadcasted_iota(jnp.int32, x.shape, 1)
    first_half = (lane % ROPE_DIM) < (ROPE_DIM // 2)
    rot = jnp.where(first_half, -pltpu.roll(x, LANES - ROPE_DIM // 2, 1), pltpu.roll(x, ROPE_DIM // 2, 1))
    return x * cos + rot * sin


def _dot(a, b):
    return jnp.dot(a, b, preferred_element_type=F32)


def _dot_nt(a, b):
    return lax.dot_general(a, b, (((1,), (1,)), ((), ())), preferred_element_type=F32)


def _rope_table_kernel(pos_ref, inv_ref, cos_ref, sin_ref):
    ang = pos_ref[...].astype(F32) * inv_ref[...]
    cos_ref[...] = jnp.cos(ang)
    sin_ref[...] = jnp.sin(ang)


def _rope_tables(positions, d):
    t = d.tokens
    half = ROPE_DIM // 2
    inv_freq = 1.0 / (ROPE_THETA ** (jnp.arange(0, ROPE_DIM, 2, dtype=F32) / ROPE_DIM))
    inv = jnp.tile(inv_freq, LANES // half).reshape(1, LANES)
    tb = min(t, 1024)
    return pl.pallas_call(
        _rope_table_kernel,
        out_shape=(jax.ShapeDtypeStruct((t, LANES), F32),) * 2,
        grid=(_exact_div(t, tb),),
        in_specs=[pl.BlockSpec((tb, 1), lambda i: (i, 0)), pl.BlockSpec((1, LANES), lambda i: (0, 0))],
        out_specs=(pl.BlockSpec((tb, LANES), lambda i: (i, 0)),) * 2,
        name="rope_tables",
    )(positions.reshape(t, 1), inv)


def _rmsnorm_kernel(x_ref, g_ref, o_ref):
    o_ref[...] = _rms(x_ref[...].astype(F32), g_ref[...]).astype(o_ref.dtype)


def _rmsnorm(x, g, out_dtype, tm):
    m, dd = x.shape
    return pl.pallas_call(
        _rmsnorm_kernel,
        out_shape=jax.ShapeDtypeStruct((m, dd), out_dtype),
        grid=(_exact_div(m, tm),),
        in_specs=[pl.BlockSpec((tm, dd), lambda i: (i, 0)), pl.BlockSpec((1, dd), lambda i: (0, 0))],
        out_specs=pl.BlockSpec((tm, dd), lambda i: (i, 0)),
        compiler_params=_cparams(("parallel",), 32),
        name="rmsnorm",
    )(x, g.reshape(1, dd).astype(F32))


def _mm_kernel(*refs, pre, post, n_extra):
    a_ref, w_ref = refs[0], refs[1]
    extras = refs[2:2 + n_extra]
    out_ref = refs[2 + n_extra]
    a = a_ref[...]
    if pre is not None:
        a = pre(a, extras)
    post(_dot(a, w_ref[...]), extras, out_ref)


def _matmul(a, w, *, k, a_col, tm, tn, out_dtype, post, pre=None, extras=(), vmem_mib=48, name):
    m = a.shape[0]
    n = w.shape[1]
    assert w.shape[0] == k
    in_specs = [pl.BlockSpec((tm, k), lambda i, j: (i, a_col)), pl.BlockSpec((k, tn), lambda i, j: (0, j))]
    in_specs += [s for _, s in extras]
    return pl.pallas_call(
        functools.partial(_mm_kernel, pre=pre, post=post, n_extra=len(extras)),
        out_shape=jax.ShapeDtypeStruct((m, n), out_dtype),
        grid=(_exact_div(m, tm), _exact_div(n, tn)),
        in_specs=in_specs,
        out_specs=pl.BlockSpec((tm, tn), lambda i, j: (i, j)),
        compiler_params=_cparams(("parallel", "arbitrary"), vmem_mib),
        name=name,
    )(a, w, *[x for x, _ in extras])


def _post_cast(acc, extras, o_ref):
    o_ref[...] = acc.astype(o_ref.dtype)


def _pre_rms(a, extras):
    return _rms(a.astype(F32), extras[0][...]).astype(BF16)


def _post_residual(acc, extras, o_ref):
    o_ref[...] = extras[0][...] + acc


def _post_in_proj(acc, extras, o_ref, *, j_rope0, j_k, j_rope1, tn, q_scale):
    cos_ref, sin_ref = extras
    j = pl.program_id(1)
    is_rope = (j >= j_rope0) & (j < j_rope1)

    @pl.when(is_rope)
    def _():
        scale = jnp.where(j < j_k, q_scale, 1.0).astype(F32)
        cos = cos_ref[...]
        sin = sin_ref[...]
        for c in range(tn // LANES):
            sl = slice(c * LANES, (c + 1) * LANES)
            o_ref[:, sl] = (_rope128(acc[:, sl], cos, sin) * scale).astype(o_ref.dtype)

    @pl.when(jnp.logical_not(is_rope))
    def _():
        o_ref[...] = acc.astype(o_ref.dtype)


def _post_rope_all(acc, extras, o_ref):
    cos_ref, sin_ref = extras
    cos = cos_ref[...]
    sin = sin_ref[...]
    for c in range(acc.shape[1] // LANES):
        sl = slice(c * LANES, (c + 1) * LANES)
        o_ref[:, sl] = _rope128(acc[:, sl], cos, sin).astype(o_ref.dtype)


def _post_mla_q(acc, extras, o_ref, *, scale):
    _, cos_ref, sin_ref = extras
    cos = cos_ref[...]
    sin = sin_ref[...]
    for hd in range(acc.shape[1] // MLA_HEAD_PAD):
        lo = slice(hd * MLA_HEAD_PAD, hd * MLA_HEAD_PAD + LANES)
        hi = slice(hd * MLA_HEAD_PAD + LANES, (hd + 1) * MLA_HEAD_PAD)
        o_ref[:, lo] = (acc[:, lo] * scale).astype(o_ref.dtype)
        o_ref[:, hi] = (_rope128(acc[:, hi], cos, sin) * scale).astype(o_ref.dtype)


def _merge_kernel(oa_ref, wpa_ref, ob_ref, wpb_ref, ga_ref, gb_ref, o_ref):
    pa = _dot(oa_ref[...], wpa_ref[...])
    pb = _dot(ob_ref[...], wpb_ref[...])
    sa = 1.0 / (1.0 + jnp.exp(-ga_ref[...].astype(F32)))
    sb = 1.0 / (1.0 + jnp.exp(-gb_ref[...].astype(F32)))
    o_ref[...] = (sa * pa + sb * pb).astype(o_ref.dtype)


def _merge(o_a, w_pa, o_b, w_pb, z, d):
    t = d.tokens
    tm, tn = d.tm, d.tn
    ka, kb = o_a.shape[1], o_b.shape[1]
    ja, jb = _exact_div(d.off_ga, tn), _exact_div(d.off_gb, tn)
    return pl.pallas_call(
        _merge_kernel,
        out_shape=jax.ShapeDtypeStruct((t, d.d_model), BF16),
        grid=(_exact_div(t, tm), _exact_div(d.d_model, tn)),
        in_specs=[
            pl.BlockSpec((tm, ka), lambda i, j: (i, 0)),
            pl.BlockSpec((ka, tn), lambda i, j: (0, j)),
            pl.BlockSpec((tm, kb), lambda i, j: (i, 0)),
            pl.BlockSpec((kb, tn), lambda i, j: (0, j)),
            pl.BlockSpec((tm, tn), lambda i, j: (i, j + ja)),
            pl.BlockSpec((tm, tn), lambda i, j: (i, j + jb)),
        ],
        out_specs=pl.BlockSpec((tm, tn), lambda i, j: (i, j)),
        compiler_params=_cparams(("parallel", "arbitrary"), 48),
        name="gated_merge",
    )(o_a, w_pa, o_b, w_pb, z, z)


def _mla_kernel(q_ref, kv_ref, kpe_ref, o_ref, k_scr, *, seq, tq):
    k_scr[:, :NOPE_DIM] = kv_ref[:, :NOPE_DIM]
    k_scr[:, NOPE_DIM:] = kpe_ref[...]
    for i in range(seq // tq):
        ln = (i + 1) * tq
        q = q_ref[i * tq:(i + 1) * tq, :]
        s = _dot_nt(q, k_scr[:ln, :])
        row = lax.broadcasted_iota(jnp.int32, (tq, ln), 0) + i * tq
        col = lax.broadcasted_iota(jnp.int32, (tq, ln), 1)
        s = jnp.where(col <= row, s, NEG_INF)
        m = jnp.max(s, axis=-1, keepdims=True)
        p = jnp.exp(s - m)
        l = jnp.sum(p, axis=-1, keepdims=True)
        o = _dot(p.astype(BF16), kv_ref[:ln, NOPE_DIM:])
        o_ref[i * tq:(i + 1) * tq, :] = (o / l).astype(o_ref.dtype)


def _mla_attention(q_full, kv, kpe, d):
    t = d.tokens
    return pl.pallas_call(
        functools.partial(_mla_kernel, seq=d.seq, tq=d.tq),
        out_shape=jax.ShapeDtypeStruct((t, d.mla_heads * MLA_V_DIM), BF16),
        grid=(d.batch, d.mla_heads),
        in_specs=[
            pl.BlockSpec((d.seq, MLA_HEAD_PAD), lambda b, h: (b, h)),
            pl.BlockSpec((d.seq, NOPE_DIM + MLA_V_DIM), lambda b, h: (b, h)),
            pl.BlockSpec((d.seq, LANES), lambda b, h: (b, 0)),
        ],
        out_specs=pl.BlockSpec((d.seq, MLA_V_DIM), lambda b, h: (b, h)),
        scratch_shapes=[pltpu.VMEM((d.seq, MLA_HEAD_PAD), BF16)],
        compiler_params=_cparams(("parallel", "parallel"), 48),
        name="mla_attention",
    )(q_full, kv, kpe)


def _swa_kernel(sink_ref, q_ref, k_ref, v_ref, o_ref, klo, khi, vlo, vhi, *, seq, window, heads_per_step):
    pair = pl.program_id(1)
    w = window
    lane = lax.broadcasted_iota(jnp.int32, (seq, LANES), 1)
    low = lane < SWA_HD
    for src_ref, lo_ref, hi_ref in ((k_ref, klo, khi), (v_ref, vlo, vhi)):
        x = src_ref[...].astype(F32)
        xs = pltpu.roll(x, SWA_HD, 1)
        zero = jnp.zeros_like(x)
        lo_ref[0] = jnp.where(low, x, zero).astype(BF16)
        hi_ref[0] = jnp.where(low, zero, xs).astype(BF16)
        lo_ref[1] = jnp.where(low, xs, zero).astype(BF16)
        hi_ref[1] = jnp.where(low, zero, x).astype(BF16)

    n_tiles = heads_per_step * SWA_HD // LANES
    tiles_per_kv = n_tiles // 2

    def block(r0, k0, klen, is_first):
        m_rows = tiles_per_kv * w
        qq = lax.broadcasted_iota(jnp.int32, (m_rows, klen), 0) % w
        kk = lax.broadcasted_iota(jnp.int32, (m_rows, klen), 1)
        mask = (kk <= qq) if is_first else ((kk > qq) & (kk <= qq + w))
        row = lax.broadcasted_iota(jnp.int32, (m_rows, 1), 0)
        for g in range(2):
            tiles = range(g * tiles_per_kv, (g + 1) * tiles_per_kv)
            q = jnp.concatenate([q_ref[pl.ds(r0, w), c * LANES:(c + 1) * LANES] for c in tiles], axis=0)
            kcat = jnp.concatenate([klo[g, pl.ds(k0, klen), :], khi[g, pl.ds(k0, klen), :]], axis=0)
            vcat = jnp.concatenate([vlo[g, pl.ds(k0, klen), :], vhi[g, pl.ds(k0, klen), :]], axis=0)
            s = _dot_nt(q, kcat)
            probs = []
            for half in range(2):
                sink = jnp.zeros((m_rows, 1), F32)
                for ti, c in enumerate(tiles):
                    head_sink = sink_ref[pair * heads_per_step + 2 * c + half]
                    sink = jnp.where((row >= ti * w) & (row < (ti + 1) * w), head_sink, sink)
                sh = jnp.where(mask, s[:, half * klen:(half + 1) * klen], NEG_INF)
                m = jnp.maximum(jnp.max(sh, axis=-1, keepdims=True), sink)
                p = jnp.exp(sh - m)
                den = jnp.sum(p, axis=-1, keepdims=True) + jnp.exp(sink - m)
                probs.append((p * (1.0 / den)).astype(BF16))
            o = _dot(jnp.concatenate(probs, axis=1), vcat)
            for ti, c in enumerate(tiles):
                o_ref[pl.ds(r0, w), c * LANES:(c + 1) * LANES] = o[ti * w:(ti + 1) * w].astype(o_ref.dtype)

    block(0, 0, w, True)

    def body(n, carry):
        r0 = pl.multiple_of(n * w, w)
        block(r0, pl.multiple_of(r0 - w, w), 2 * w, False)
        return carry

    lax.fori_loop(1, seq // w, body, 0)


def _swa_attention(z, sinks, d):
    t = d.tokens
    hps = 2 * (d.swa_heads // d.swa_kv_heads)
    qw = hps * SWA_HD
    n_pairs = _exact_div(d.swa_kv_heads, 2)
    jq, jk, jv = _exact_div(d.off_qs, qw), _exact_div(d.off_ks, LANES), _exact_div(d.off_vs, LANES)
    grid_spec = pltpu.PrefetchScalarGridSpec(
        num_scalar_prefetch=1,
        grid=(d.batch, n_pairs),
        in_specs=[
            pl.BlockSpec((d.seq, qw), lambda b, p, s: (b, jq + p)),
            pl.BlockSpec((d.seq, LANES), lambda b, p, s: (b, jk + p)),
            pl.BlockSpec((d.seq, LANES), lambda b, p, s: (b, jv + p)),
        ],
        out_specs=pl.BlockSpec((d.seq, qw), lambda b, p, s: (b, p)),
        scratch_shapes=[pltpu.VMEM((2, d.seq, LANES), BF16)] * 4,
    )
    return pl.pallas_call(
        functools.partial(_swa_kernel, seq=d.seq, window=d.window, heads_per_step=hps),
        out_shape=jax.ShapeDtypeStruct((t, d.swa_q), BF16),
        grid_spec=grid_spec,
        compiler_params=_cparams(("parallel", "parallel"), 48),
        name="swa_attention",
    )(sinks.astype(F32), z, z, z)


def _cross_router_kernel(h_ref, gc_ref, wq_ref, kvm_ref, wo_ref, gf_ref, wr_ref, br_ref,
                         h2_ref, xnf_ref, rt_ref, cnt_ref, run_ref, *, x_heads, n_groups, scale):
    tm = h_ref.shape[0]
    hx = x_heads * X_HD
    n_exp = n_groups * EPG

    @pl.when(pl.program_id(0) == 0)
    def _():
        run_ref[...] = jnp.zeros_like(run_ref)

    h = h_ref[...]
    hn = _rms(h, gc_ref[...]).astype(BF16)
    q = (_dot(hn, wq_ref[...]) * scale).astype(BF16)
    outs = []
    for hd in range(x_heads):
        kh = kvm_ref[:, hd * X_HD:(hd + 1) * X_HD]
        vh = kvm_ref[:, hx + hd * X_HD:hx + (hd + 1) * X_HD]
        s = _dot_nt(q[:, hd * X_HD:(hd + 1) * X_HD], kh)
        m = jnp.max(s, axis=-1, keepdims=True)
        p = jnp.exp(s - m)
        l = jnp.sum(p, axis=-1, keepdims=True)
        outs.append((_dot(p.astype(BF16), vh) / l).astype(BF16))
    h2 = h + _dot(jnp.concatenate(outs, axis=1), wo_ref[...])
    h2_ref[...] = h2
    xnf = _rms(h2, gf_ref[...])
    xnf_ref[...] = xnf

    logits = _dot(xnf.astype(BF16), wr_ref[...]) + br_ref[...]
    lane = lax.broadcasted_iota(jnp.int32, (tm, LANES), 1).astype(F32)
    big = float(LANES)
    is_group = (lane >= n_exp) & (lane < n_exp + n_groups)
    gl = jnp.where(is_group, logits, -jnp.inf)
    gmax = jnp.max(gl, axis=-1, keepdims=True)
    g_lane = jnp.min(jnp.where(gl == gmax, lane, big), axis=-1, keepdims=True)
    p_group = 1.0 / jnp.sum(jnp.where(is_group, jnp.exp(gl - gmax), 0.0), axis=-1, keepdims=True)
    e_lo = (g_lane - n_exp) * EPG
    in_group = (lane >= e_lo) & (lane < e_lo + EPG)
    el = jnp.where(in_group, logits, -jnp.inf)
    m1 = jnp.max(el, axis=-1, keepdims=True)
    i1 = jnp.min(jnp.where(el == m1, lane, big), axis=-1, keepdims=True)
    el2 = jnp.where(lane == i1, -jnp.inf, el)
    m2 = jnp.max(el2, axis=-1, keepdims=True)
    i2 = jnp.min(jnp.where(el2 == m2, lane, big), axis=-1, keepdims=True)
    w2 = jnp.exp(m2 - m1)
    gate1 = p_group / (1.0 + w2)
    gate2 = gate1 * w2

    hot1 = lane == i1
    hot2 = lane == i2
    onehot = jnp.where(hot1 | hot2, 1.0, 0.0)
    rr = lax.broadcasted_iota(jnp.int32, (tm, tm), 0)
    cc = lax.broadcasted_iota(jnp.int32, (tm, tm), 1)
    tri = jnp.where(cc < rr, 1.0, 0.0).astype(BF16)
    before = _dot(tri, onehot.astype(BF16)) + run_ref[...]
    rank1 = jnp.sum(jnp.where(hot1, before, 0.0), axis=-1, keepdims=True)
    rank2 = jnp.sum(jnp.where(hot2, before, 0.0), axis=-1, keepdims=True)
    run = run_ref[...] + jnp.sum(onehot, axis=0, keepdims=True)
    run_ref[...] = run
    cnt_ref[...] = run

    rt = jnp.where(lane == 0, i1, 0.0)
    rt = jnp.where(lane == 1, i2, rt)
    rt = jnp.where(lane == 2, gate1, rt)
    rt = jnp.where(lane == 3, gate2, rt)
    rt = jnp.where(lane == 4, rank1, rt)
    rt = jnp.where(lane == 5, rank2, rt)
    rt_ref[...] = rt


def _cross_router(h, g_cross, w_xq, kvm, w_xo, g_ffn, w_r, b_r, d):
    t = d.tokens
    tm = d.t_cross
    dm = d.d_model
    hx = d.x_heads * X_HD
    steps_per_batch = _exact_div(d.seq, tm)
    row = lambda i: (i, 0)
    fixed = lambda i: (0, 0)
    return pl.pallas_call(
        functools.partial(_cross_router_kernel, x_heads=d.x_heads, n_groups=d.n_groups, scale=X_HD ** -0.5),
        out_shape=(
            jax.ShapeDtypeStruct((t, dm), F32),
            jax.ShapeDtypeStruct((t, dm), F32),
            jax.ShapeDtypeStruct((t, LANES), F32),
            jax.ShapeDtypeStruct((1, LANES), F32),
        ),
        grid=(_exact_div(t, tm),),
        in_specs=[
            pl.BlockSpec((tm, dm), row),
            pl.BlockSpec((1, dm), fixed),
            pl.BlockSpec((dm, hx), fixed),
            pl.BlockSpec((d.mem_len, 2 * hx), lambda i: (i // steps_per_batch, 0)),
            pl.BlockSpec((hx, dm), fixed),
            pl.BlockSpec((1, dm), fixed),
            pl.BlockSpec((dm, LANES), fixed),
            pl.BlockSpec((1, LANES), fixed),
        ],
        out_specs=(
            pl.BlockSpec((tm, dm), row),
            pl.BlockSpec((tm, dm), row),
            pl.BlockSpec((tm, LANES), row),
            pl.BlockSpec((1, LANES), fixed),
        ),
        scratch_shapes=[pltpu.VMEM((1, LANES), F32)],
        compiler_params=_cparams(("arbitrary",), 56),
        name="cross_attention_router",
    )(h, g_cross.reshape(1, dm), w_xq, kvm, w_xo, g_ffn.reshape(1, dm), w_r, b_r)


def _row_copy(src_hbm, src_row, dst_ref, dst_row, sem):
    return pltpu.make_async_copy(src_hbm.at[pl.ds(src_row, 1)], dst_ref.at[pl.ds(dst_row, 1)], sem)


def _dispatch_kernel(dest_ref, x_ref, init_hbm, xs_hbm, sem, *, t_tok):
    del init_hbm
    base = pl.program_id(0) * t_tok

    def issue(r, carry):
        for k in range(TOP_K):
            _row_copy(x_ref, r, xs_hbm, dest_ref[TOP_K * (base + r) + k], sem).start()
        return carry

    def drain(r, carry):
        for k in range(TOP_K):
            _row_copy(x_ref, 0, xs_hbm, 0, sem).wait()
        return carry

    lax.fori_loop(0, t_tok, issue, 0)
    lax.fori_loop(0, t_tok, drain, 0)


def _dispatch(dest, x, d):
    t = d.tokens
    p_rows = d.n_blocks * d.moe_block
    init = jnp.zeros((p_rows, d.d_model), x.dtype)
    grid_spec = pltpu.PrefetchScalarGridSpec(
        num_scalar_prefetch=1,
        grid=(_exact_div(t, d.t_tok),),
        in_specs=[pl.BlockSpec((d.t_tok, d.d_model), lambda i, dest: (i, 0)), pl.BlockSpec(memory_space=pl.ANY)],
        out_specs=pl.BlockSpec(memory_space=pl.ANY),
        scratch_shapes=[pltpu.SemaphoreType.DMA(())],
    )
    return pl.pallas_call(
        functools.partial(_dispatch_kernel, t_tok=d.t_tok),
        out_shape=jax.ShapeDtypeStruct((p_rows, d.d_model), x.dtype),
        grid_spec=grid_spec,
        input_output_aliases={2: 0},
        compiler_params=_cparams(("arbitrary",), 32),
        name="moe_dispatch",
    )(dest, x, init)


W_SLABS = 4


def _moe_steps(d):
    return W_SLABS + d.n_blocks + W_SLABS * d.n_experts + 1


def _moe_schedule(padded, d):
    bm, ne, nblk = d.moe_block, d.n_experts, d.n_blocks
    i32 = jnp.int32
    nb = padded // bm
    first_blk = (jnp.cumsum(padded) - padded) // bm
    n_used = jnp.sum(nb)
    per_expert = jnp.where(jnp.arange(ne) == ne - 1, nb, jnp.maximum(nb, W_SLABS))
    step_end = W_SLABS + jnp.cumsum(per_expert)
    step_start = step_end - per_expert
    total = step_end[-1]
    s = jnp.arange(_moe_steps(d), dtype=i32)
    is_pro = s < W_SLABS
    is_tail = s >= total
    e = jnp.minimum(jnp.sum((step_end[None, :] <= s[:, None]).astype(i32), axis=1), ne - 1)
    j = jnp.where(is_pro, s, s - step_start[e])
    has_blk = jnp.logical_not(is_pro | is_tail) & (j < nb[e])
    blk = jnp.where(is_pro, 0, first_blk[e] + jnp.minimum(j, nb[e]))
    xblk = jnp.where(has_blk, blk, 0)
    oblk = jnp.where(is_tail, jnp.minimum(n_used + s - total, nblk), blk)
    nxt = jnp.where(is_pro, 0, jnp.minimum(e + 1, ne - 1))
    slab = jnp.where(is_tail, W_SLABS - 1, jnp.minimum(j, W_SLABS - 1))
    slot = jnp.where(is_pro, 1, e % 2)
    return tuple(v.astype(i32) for v in (xblk, oblk, nxt, slab, slot, is_tail))


def _silu_mul(a, b):
    return (a / (1.0 + jnp.exp(-a))) * b


def _expert_up_kernel(xblk, oblk, nxt, slab, slot, tail, x_ref, wg_st, wu_st, hb_ref, wg0, wg1, wu0, wu1):
    del xblk, oblk, nxt
    s = pl.program_id(0)
    rows = wg_st.shape[1]

    @pl.when(s == 0)
    def _():
        wg1[...] = jnp.zeros_like(wg1)
        wu1[...] = jnp.zeros_like(wu1)

    @pl.when(tail[s] == 1)
    def _():
        hb_ref[...] = jnp.zeros_like(hb_ref)

    r0 = pl.multiple_of(slab[s] * rows, rows)
    for cur, (wg_c, wu_c, wg_n, wu_n) in enumerate(((wg0, wu0, wg1, wu1), (wg1, wu1, wg0, wu0))):
        @pl.when((tail[s] == 0) & (slot[s] == cur))
        def _():
            wg_n[pl.ds(r0, rows), :] = wg_st[0].astype(BF16)
            wu_n[pl.ds(r0, rows), :] = wu_st[0].astype(BF16)
            x = x_ref[...].astype(BF16)
            hb_ref[...] = _silu_mul(_dot(x, wg_c[...]), _dot(x, wu_c[...])).astype(hb_ref.dtype)


def _expert_down_kernel(xblk, oblk, nxt, slab, slot, tail, hb_ref, wd_st, y_ref, wd0, wd1):
    del xblk, oblk, nxt
    s = pl.program_id(0)
    rows = wd_st.shape[1]

    @pl.when(s == 0)
    def _():
        wd1[...] = jnp.zeros_like(wd1)

    @pl.when(tail[s] == 1)
    def _():
        y_ref[...] = jnp.zeros_like(y_ref)

    r0 = pl.multiple_of(slab[s] * rows, rows)
    for cur, (wd_c, wd_n) in enumerate(((wd0, wd1), (wd1, wd0))):
        @pl.when((tail[s] == 0) & (slot[s] == cur))
        def _():
            wd_n[pl.ds(r0, rows), :] = wd_st[0].astype(BF16)
            y_ref[...] = _dot(hb_ref[...], wd_c[...])


def _experts(sched, xs, w_gate, w_up, w_down, layer, d):
    bm = d.moe_block
    dm, de = d.d_model, d.d_expert
    ne = d.n_experts
    rows_out = (d.n_blocks + 1) * bm
    w_gate, w_up = w_gate.reshape(-1, dm, de), w_up.reshape(-1, dm, de)
    w_down = w_down.reshape(-1, de, dm)
    x_map = lambda s, xb, ob, nx, sl, st, tl: (xb[s], 0)
    o_map = lambda s, xb, ob, nx, sl, st, tl: (ob[s], 0)
    w_map = lambda s, xb, ob, nx, sl, st, tl: (layer * ne + nx[s], sl[s], 0)
    ku, kd = _exact_div(dm, W_SLABS), _exact_div(de, W_SLABS)
    hb = pl.pallas_call(
        _expert_up_kernel,
        out_shape=jax.ShapeDtypeStruct((rows_out, de), BF16),
        grid_spec=pltpu.PrefetchScalarGridSpec(
            num_scalar_prefetch=len(sched),
            grid=(_moe_steps(d),),
            in_specs=[pl.BlockSpec((bm, dm), x_map), pl.BlockSpec((1, ku, de), w_map),
                      pl.BlockSpec((1, ku, de), w_map)],
            out_specs=pl.BlockSpec((bm, de), o_map),
            scratch_shapes=[pltpu.VMEM((dm, de), BF16)] * 4,
        ),
        compiler_params=_cparams(("arbitrary",), 56),
        name="moe_experts_up",
    )(*sched, xs, w_gate, w_up)
    return pl.pallas_call(
        _expert_down_kernel,
        out_shape=jax.ShapeDtypeStruct((rows_out, dm), F32),
        grid_spec=pltpu.PrefetchScalarGridSpec(
            num_scalar_prefetch=len(sched),
            grid=(_moe_steps(d),),
            in_specs=[pl.BlockSpec((bm, de), x_map), pl.BlockSpec((1, kd, dm), w_map)],
            out_specs=pl.BlockSpec((bm, dm), o_map),
            scratch_shapes=[pltpu.VMEM((de, dm), BF16)] * 2,
        ),
        compiler_params=_cparams(("arbitrary",), 40),
        name="moe_experts_down",
    )(*sched, hb, w_down)


def _combine_kernel(dest_ref, y_hbm, h_ref, rt_ref, g_ref, *rest, t_tok, last):
    if last:
        out_ref, ybuf, sem = rest
    else:
        h3_ref, out_ref, ybuf, sem = rest
    base = pl.program_id(0) * t_tok

    def issue(r, carry):
        for k in range(TOP_K):
            _row_copy(y_hbm, dest_ref[TOP_K * (base + r) + k], ybuf.at[k], r, sem).start()
        return carry

    def drain(r, carry):
        for k in range(TOP_K):
            _row_copy(y_hbm, 0, ybuf.at[k], 0, sem).wait()
        return carry

    lax.fori_loop(0, t_tok, issue, 0)
    lax.fori_loop(0, t_tok, drain, 0)
    rt = rt_ref[...]
    h3 = h_ref[...] + rt[:, 2:3] * ybuf[0] + rt[:, 3:4] * ybuf[1]
    if not last:
        h3_ref[...] = h3
    out_ref[...] = _rms(h3, g_ref[...]).astype(out_ref.dtype)


def _combine(dest, y, h2, rt, g_next, d, last):
    t = d.tokens
    tt = d.t_tok
    dm = d.d_model
    row = lambda i, ds: (i, 0)
    out_specs = pl.BlockSpec((tt, dm), row)
    if last:
        out_shape = jax.ShapeDtypeStruct((t, dm), F32)
    else:
        out_shape = (jax.ShapeDtypeStruct((t, dm), F32), jax.ShapeDtypeStruct((t, dm), BF16))
        out_specs = (out_specs, pl.BlockSpec((tt, dm), row))
    grid_spec = pltpu.PrefetchScalarGridSpec(
        num_scalar_prefetch=1,
        grid=(_exact_div(t, tt),),
        in_specs=[
            pl.BlockSpec(memory_space=pl.ANY),
            pl.BlockSpec((tt, dm), row),
            pl.BlockSpec((tt, LANES), row),
            pl.BlockSpec((1, dm), lambda i, ds: (0, 0)),
        ],
        out_specs=out_specs,
        scratch_shapes=[pltpu.VMEM((TOP_K, tt, dm), F32), pltpu.SemaphoreType.DMA(())],
    )
    return pl.pallas_call(
        functools.partial(_combine_kernel, t_tok=tt, last=last),
        out_shape=out_shape,
        grid_spec=grid_spec,
        compiler_params=_cparams(("arbitrary",), 48),
        name="moe_combine",
    )(dest, y, h2, rt, g_next.reshape(1, dm))


def _layer(h, xn, cos, sin, memn, p, d, g_next, last):
    t = d.tokens
    dm = d.d_model
    tm = d.tm
    row128 = pl.BlockSpec((tm, LANES), lambda i, j: (i, 0))

    tn = d.tn_in
    z = _matmul(
        xn, p["w_main"], k=dm, a_col=0, tm=tm, tn=tn, out_dtype=BF16,
        post=functools.partial(_post_in_proj, j_rope0=_exact_div(d.off_qs, tn), j_k=_exact_div(d.off_ks, tn),
                               j_rope1=_exact_div(d.off_vs, tn), tn=tn, q_scale=SWA_HD ** -0.5),
        extras=((cos, row128), (sin, row128)), name="in_proj")
    kpe = _matmul(xn, p["w_kr"], k=dm, a_col=0, tm=tm, tn=LANES, out_dtype=BF16, post=_post_rope_all,
                  extras=((cos, row128), (sin, row128)), name="rope_key_proj")

    tn_q = 2 * MLA_HEAD_PAD
    q_full = _matmul(
        z, p["w_uq"], k=d.q_lora, a_col=0, tm=tm, tn=tn_q, out_dtype=BF16, pre=_pre_rms,
        post=functools.partial(_post_mla_q, scale=(NOPE_DIM + ROPE_DIM) ** -0.5),
        extras=((p["g_qa"], pl.BlockSpec((1, d.q_lora), lambda i, j: (0, 0))), (cos, row128), (sin, row128)),
        name="mla_q_proj")
    kv = _matmul(
        z, p["w_ukv"], k=d.kv_lora, a_col=_exact_div(d.off_ckv, d.kv_lora), tm=tm, tn=d.tn, out_dtype=BF16,
        pre=_pre_rms, post=_post_cast,
        extras=((p["g_kva"], pl.BlockSpec((1, d.kv_lora), lambda i, j: (0, 0))),), name="mla_kv_proj")
    o_a = _mla_attention(q_full, kv, kpe, d)

    o_b = _swa_attention(z, p["sinks"], d)

    merged = _merge(o_a, p["w_pa"], o_b, p["w_pb"], z, d)
    h1 = _matmul(merged, p["w_o"], k=dm, a_col=0, tm=tm, tn=d.tn, out_dtype=F32, post=_post_residual,
                 extras=((h, pl.BlockSpec((tm, d.tn), lambda i, j: (i, j))),), name="out_proj")

    kvm = _matmul(memn, p["w_xkv"], k=dm, a_col=0, tm=memn.shape[0], tn=d.tn, out_dtype=BF16,
                  post=_post_cast, name="mem_kv_proj")
    h2, xnf, rt, cnt = _cross_router(h1, p["g_cross"], p["w_xq"], kvm, p["w_xo"], p["g_ffn"],
                                     p["w_r"], p["b_r"], d)

    bm = d.moe_block
    counts = cnt[0, :d.n_experts].astype(jnp.int32)
    padded = (counts + bm - 1) // bm * bm
    pad_ends = jnp.cumsum(padded)
    pad_starts = pad_ends - padded
    expert = rt[:, 0:TOP_K].astype(jnp.int32)
    dest = (pad_starts[expert] + rt[:, 4:4 + TOP_K].astype(jnp.int32)).reshape(t * TOP_K)

    xs = _dispatch(dest, xnf, d)
    y = _experts(_moe_schedule(padded, d), xs, p["w_gate"], p["w_up"], p["w_down"], p["layer"], d)
    return _combine(dest, y, h2, rt, g_next, d, last)


def _prep_layer(l, d, g_mix, w_in, g_qa, g_kva, w_uq, w_ukv, sinks, w_pa, w_pb, w_o, g_cross, w_xq, w_xkv,
                w_xo, g_ffn, w_group, b_group, w_router, b_router, w_gate, w_up, w_down):
    dm = d.d_model
    kr0 = d.q_lora + d.kv_lora
    w = w_in[l]
    w_main = jnp.concatenate([w[:, :kr0], w[:, kr0 + ROPE_DIM:]], axis=1).astype(BF16)
    w_kr = jnp.pad(w[:, kr0:kr0 + ROPE_DIM], ((0, 0), (0, LANES - ROPE_DIM))).astype(BF16)
    qk = NOPE_DIM + ROPE_DIM
    wq = w_uq[l].reshape(d.q_lora, d.mla_heads, qk)
    wq = jnp.pad(wq, ((0, 0), (0, 0), (0, MLA_HEAD_PAD - qk))).reshape(d.q_lora, d.mla_heads * MLA_HEAD_PAD)
    n_r = d.n_experts + d.n_groups
    w_r = jnp.pad(jnp.concatenate([w_router[l], w_group[l]], axis=1), ((0, 0), (0, LANES - n_r)))
    b_r = jnp.pad(jnp.concatenate([b_router[l], b_group[l]]), (0, LANES - n_r)).reshape(1, LANES)
    return dict(
        w_main=w_main, w_kr=w_kr, w_uq=wq.astype(BF16), w_ukv=w_ukv[l].astype(BF16),
        g_qa=g_qa[l].reshape(1, -1).astype(F32), g_kva=g_kva[l].reshape(1, -1).astype(F32),
        sinks=sinks[l], w_pa=w_pa[l].astype(BF16), w_pb=w_pb[l].astype(BF16), w_o=w_o[l].astype(BF16),
        g_cross=g_cross[l].astype(F32), w_xq=w_xq[l].astype(BF16), w_xkv=w_xkv[l].astype(BF16),
        w_xo=w_xo[l].astype(BF16), g_ffn=g_ffn[l].astype(F32), w_r=w_r.astype(BF16), b_r=b_r.astype(F32),
        w_gate=w_gate, w_up=w_up, w_down=w_down, layer=l,
    )


def _forward(d, x, mem, positions, g_mix, w_in, g_qa, g_kva, w_uq, w_ukv, sinks, w_pa, w_pb, w_o,
             g_cross, g_mem, w_xq, w_xkv, w_xo, g_ffn, w_group, b_group, w_router, b_router,
             w_gate, w_up, w_down, g_final):
    depth = w_in.shape[0]
    t = d.tokens
    dm = d.d_model
    cos, sin = _rope_tables(positions, d)
    memn = _rmsnorm(mem.reshape(d.batch * d.mem_len, dm), g_mem, BF16, d.t_norm)
    h = x.reshape(t, dm)
    xn = _rmsnorm(h, g_mix[0], BF16, d.t_norm)
    for l in range(depth):
        p = _prep_layer(l, d, g_mix, w_in, g_qa, g_kva, w_uq, w_ukv, sinks, w_pa, w_pb, w_o, g_cross,
                        w_xq, w_xkv, w_xo, g_ffn, w_group, b_group, w_router, b_router, w_gate, w_up, w_down)
        last = l == depth - 1
        g_next = g_final if last else g_mix[l + 1]
        res = _layer(h, xn, cos, sin, memn, p, d, g_next, last)
        if last:
            return res.reshape(d.batch, d.seq, dm)
        h, xn = res


def kernel(x, mem, positions, g_mix, w_in, g_qa, g_kva, w_uq, w_ukv, sinks, w_pa, w_pb, w_o, g_cross, g_mem,
           w_xq, w_xkv, w_xo, g_ffn, w_group, b_group, w_router, b_router, w_gate, w_up, w_down, g_final):
    return _forward(Dims(), x, mem, positions, g_mix, w_in, g_qa, g_kva, w_uq, w_ukv, sinks, w_pa, w_pb, w_o,
                    g_cross, g_mem, w_xq, w_xkv, w_xo, g_ffn, w_group, b_group, w_router, b_router,
                    w_gate, w_up, w_down, g_final)
```

```python
import functools
from typing import NamedTuple

import jax
import jax.numpy as jnp
from jax import lax
from jax.experimental import pallas as pl
from jax.experimental.pallas import tpu as pltpu

F32 = jnp.float32
BF16 = jnp.bfloat16
EPS = 1e-6
ROPE_THETA = 10000.0
NEG_INF = -1e30
LANES = 128
ROPE_DIM = 64
NOPE_DIM = 128
MLA_V_DIM = 128
MLA_HEAD_PAD = 256
SWA_HD = 64
X_HD = 128
EPG = 8
TOP_K = 2
MIB = 1024 * 1024


class Dims(NamedTuple):
    batch: int = 4
    seq: int = 2048
    d_model: int = 4096
    mem_len: int = 256
    mla_heads: int = 16
    q_lora: int = 1024
    kv_lora: int = 512
    swa_heads: int = 32
    swa_kv_heads: int = 8
    window: int = 128
    x_heads: int = 4
    n_groups: int = 4
    d_expert: int = 768
    moe_block: int = 128
    tm: int = 1024
    tm_in: int = 2048
    tn_in: int = 512
    tn: int = 512
    tq: int = 256
    t_cross: int = 128
    t_tok: int = 256
    t_norm: int = 256

    @property
    def tokens(self):
        return self.batch * self.seq

    @property
    def n_experts(self):
        return self.n_groups * EPG

    @property
    def swa_q(self):
        return self.swa_heads * SWA_HD

    @property
    def swa_kv(self):
        return self.swa_kv_heads * SWA_HD

    @property
    def off_ckv(self):
        return self.q_lora

    @property
    def off_qs(self):
        return self.q_lora + self.kv_lora

    @property
    def off_ks(self):
        return self.off_qs + self.swa_q

    @property
    def off_vs(self):
        return self.off_ks + self.swa_kv

    @property
    def off_ga(self):
        return self.off_vs + self.swa_kv

    @property
    def off_gb(self):
        return self.off_ga + self.d_model

    @property
    def n_main(self):
        return self.off_gb + self.d_model

    @property
    def n_blocks(self):
        return -(-(self.tokens * TOP_K) // self.moe_block) + self.n_experts


def _exact_div(a, b):
    assert a % b == 0, (a, b)
    return a // b


def _cparams(sem, vmem_mib):
    return pltpu.CompilerParams(dimension_semantics=sem, vmem_limit_bytes=vmem_mib * MIB)


def _rms(x, g):
    return x * lax.rsqrt(jnp.mean(x * x, axis=-1, keepdims=True) + EPS) * g


def _rope128(x, cos, sin):
    lane = lax.broadcasted_iota(jnp.int32, x.shape, 1)
    first_half = (lane % ROPE_DIM) < (ROPE_DIM // 2)
    rot = jnp.where(first_half, -pltpu.roll(x, LANES - ROPE_DIM // 2, 1), pltpu.roll(x, ROPE_DIM // 2, 1))
    return x * cos + rot * sin


def _dot(a, b):
    return jnp.dot(a, b, preferred_element_type=F32)


def _dot_nt(a, b):
    return lax.dot_general(a, b, (((1,), (1,)), ((), ())), preferred_element_type=F32)


def _rope_table_kernel(pos_ref, inv_ref, cos_ref, sin_ref):
    ang = pos_ref[...].astype(F32) * inv_ref[...]
    cos_ref[...] = jnp.cos(ang)
    sin_ref[...] = jnp.sin(ang)


def _rope_tables(positions, d):
    t = d.tokens
    half = ROPE_DIM // 2
    inv_freq = 1.0 / (ROPE_THETA ** (jnp.arange(0, ROPE_DIM, 2, dtype=F32) / ROPE_DIM))
    inv = jnp.tile(inv_freq, LANES // half).reshape(1, LANES)
    tb = min(t, 1024)
    return pl.pallas_call(
        _rope_table_kernel,
        out_shape=(jax.ShapeDtypeStruct((t, LANES), F32),) * 2,
        grid=(_exact_div(t, tb),),
        in_specs=[pl.BlockSpec((tb, 1), lambda i: (i, 0)), pl.BlockSpec((1, LANES), lambda i: (0, 0))],
        out_specs=(pl.BlockSpec((tb, LANES), lambda i: (i, 0)),) * 2,
        name="rope_tables",
    )(positions.reshape(t, 1), inv)


def _rmsnorm_kernel(x_ref, g_ref, o_ref):
    o_ref[...] = _rms(x_ref[...].astype(F32), g_ref[...]).astype(o_ref.dtype)


def _rmsnorm(x, g, out_dtype, tm):
    m, dd = x.shape
    return pl.pallas_call(
        _rmsnorm_kernel,
        out_shape=jax.ShapeDtypeStruct((m, dd), out_dtype),
        grid=(_exact_div(m, tm),),
        in_specs=[pl.BlockSpec((tm, dd), lambda i: (i, 0)), pl.BlockSpec((1, dd), lambda i: (0, 0))],
        out_specs=pl.BlockSpec((tm, dd), lambda i: (i, 0)),
        compiler_params=_cparams(("parallel",), 32),
        name="rmsnorm",
    )(x, g.reshape(1, dd).astype(F32))


def _mm_kernel(*refs, pre, post, n_extra):
    a_ref, w_ref = refs[0], refs[1]
    extras = refs[2:2 + n_extra]
    out_ref = refs[2 + n_extra]
    a = a_ref[...]
    if pre is not None:
        a = pre(a, extras)
    post(_dot(a, w_ref[...]), extras, out_ref)


def _matmul(a, w, *, k, a_col, tm, tn, out_dtype, post, pre=None, extras=(), vmem_mib=48, name):
    m = a.shape[0]
    n = w.shape[1]
    assert w.shape[0] == k
    in_specs = [pl.BlockSpec((tm, k), lambda i, j: (i, a_col)), pl.BlockSpec((k, tn), lambda i, j: (0, j))]
    in_specs += [s for _, s in extras]
    return pl.pallas_call(
        functools.partial(_mm_kernel, pre=pre, post=post, n_extra=len(extras)),
        out_shape=jax.ShapeDtypeStruct((m, n), out_dtype),
        grid=(_exact_div(m, tm), _exact_div(n, tn)),
        in_specs=in_specs,
        out_specs=pl.BlockSpec((tm, tn), lambda i, j: (i, j)),
        compiler_params=_cparams(("parallel", "arbitrary"), vmem_mib),
        name=name,
    )(a, w, *[x for x, _ in extras])


def _post_cast(acc, extras, o_ref):
    o_ref[...] = acc.astype(o_ref.dtype)


def _pre_rms(a, extras):
    return _rms(a.astype(F32), extras[0][...]).astype(BF16)


def _post_residual(acc, extras, o_ref):
    o_ref[...] = extras[0][...] + acc


def _post_in_proj(acc, extras, o_ref, *, j_rope0, j_k, j_rope1, tn, q_scale):
    cos_ref, sin_ref = extras
    j = pl.program_id(1)
    is_rope = (j >= j_rope0) & (j < j_rope1)

    @pl.when(is_rope)
    def _():
        scale = jnp.where(j < j_k, q_scale, 1.0).astype(F32)
        cos = cos_ref[...]
        sin = sin_ref[...]
        for c in range(tn // LANES):
            sl = slice(c * LANES, (c + 1) * LANES)
            o_ref[:, sl] = (_rope128(acc[:, sl], cos, sin) * scale).astype(o_ref.dtype)

    @pl.when(jnp.logical_not(is_rope))
    def _():
        o_ref[...] = acc.astype(o_ref.dtype)


def _post_rope_all(acc, extras, o_ref):
    cos_ref, sin_ref = extras
    cos = cos_ref[...]
    sin = sin_ref[...]
    for c in range(acc.shape[1] // LANES):
        sl = slice(c * LANES, (c + 1) * LANES)
        o_ref[:, sl] = _rope128(acc[:, sl], cos, sin).astype(o_ref.dtype)


def _post_mla_q(acc, extras, o_ref, *, scale):
    _, cos_ref, sin_ref = extras
    cos = cos_ref[...]
    sin = sin_ref[...]
    for hd in range(acc.shape[1] // MLA_HEAD_PAD):
        lo = slice(hd * MLA_HEAD_PAD, hd * MLA_HEAD_PAD + LANES)
        hi = slice(hd * MLA_HEAD_PAD + LANES, (hd + 1) * MLA_HEAD_PAD)
        o_ref[:, lo] = (acc[:, lo] * scale).astype(o_ref.dtype)
        o_ref[:, hi] = (_rope128(acc[:, hi], cos, sin) * scale).astype(o_ref.dtype)


def _merge_kernel(oa_ref, wpa_ref, ob_ref, wpb_ref, ga_ref, gb_ref, o_ref):
    pa = _dot(oa_ref[...], wpa_ref[...])
    pb = _dot(ob_ref[...], wpb_ref[...])
    sa = 1.0 / (1.0 + jnp.exp(-ga_ref[...].astype(F32)))
    sb = 1.0 / (1.0 + jnp.exp(-gb_ref[...].astype(F32)))
    o_ref[...] = (sa * pa + sb * pb).astype(o_ref.dtype)


def _merge(o_a, w_pa, o_b, w_pb, z, d):
    t = d.tokens
    tm, tn = d.tm, d.tn
    ka, kb = o_a.shape[1], o_b.shape[1]
    ja, jb = _exact_div(d.off_ga, tn), _exact_div(d.off_gb, tn)
    return pl.pallas_call(
        _merge_kernel,
        out_shape=jax.ShapeDtypeStruct((t, d.d_model), BF16),
        grid=(_exact_div(t, tm), _exact_div(d.d_model, tn)),
        in_specs=[
            pl.BlockSpec((tm, ka), lambda i, j: (i, 0)),
            pl.BlockSpec((ka, tn), lambda i, j: (0, j)),
            pl.BlockSpec((tm, kb), lambda i, j: (i, 0)),
            pl.BlockSpec((kb, tn), lambda i, j: (0, j)),
            pl.BlockSpec((tm, tn), lambda i, j: (i, j + ja)),
            pl.BlockSpec((tm, tn), lambda i, j: (i, j + jb)),
        ],
        out_specs=pl.BlockSpec((tm, tn), lambda i, j: (i, j)),
        compiler_params=_cparams(("parallel", "arbitrary"), 48),
        name="gated_merge",
    )(o_a, w_pa, o_b, w_pb, z, z)


def _mla_kernel(q_ref, kv_ref, kpe_ref, o_ref, k_scr, *, seq, tq):
    k_scr[:, :NOPE_DIM] = kv_ref[:, :NOPE_DIM]
    k_scr[:, NOPE_DIM:] = kpe_ref[...]
    for i in range(seq // tq):
        ln = (i + 1) * tq
        q = q_ref[i * tq:(i + 1) * tq, :]
        s = _dot_nt(q, k_scr[:ln, :])
        row = lax.broadcasted_iota(jnp.int32, (tq, ln), 0) + i * tq
        col = lax.broadcasted_iota(jnp.int32, (tq, ln), 1)
        s = jnp.where(col <= row, s, NEG_INF)
        m = jnp.max(s, axis=-1, keepdims=True)
        p = jnp.exp(s - m)
        l = jnp.sum(p, axis=-1, keepdims=True)
        o = _dot(p.astype(BF16), kv_ref[:ln, NOPE_DIM:])
        o_ref[i * tq:(i + 1) * tq, :] = (o / l).astype(o_ref.dtype)


def _mla_attention(q_full, kv, kpe, d):
    t = d.tokens
    return pl.pallas_call(
        functools.partial(_mla_kernel, seq=d.seq, tq=d.tq),
        out_shape=jax.ShapeDtypeStruct((t, d.mla_heads * MLA_V_DIM), BF16),
        grid=(d.batch, d.mla_heads),
        in_specs=[
            pl.BlockSpec((d.seq, MLA_HEAD_PAD), lambda b, h: (b, h)),
            pl.BlockSpec((d.seq, NOPE_DIM + MLA_V_DIM), lambda b, h: (b, h)),
            pl.BlockSpec((d.seq, LANES), lambda b, h: (b, 0)),
        ],
        out_specs=pl.BlockSpec((d.seq, MLA_V_DIM), lambda b, h: (b, h)),
        scratch_shapes=[pltpu.VMEM((d.seq, MLA_HEAD_PAD), BF16)],
        compiler_params=_cparams(("parallel", "parallel"), 48),
        name="mla_attention",
    )(q_full, kv, kpe)


def _swa_kernel(sink_ref, q_ref, k_ref, v_ref, o_ref, klo, khi, vlo, vhi, *, seq, window, heads_per_step):
    pair = pl.program_id(1)
    w = window
    lane = lax.broadcasted_iota(jnp.int32, (seq, LANES), 1)
    low = lane < SWA_HD
    for src_ref, lo_ref, hi_ref in ((k_ref, klo, khi), (v_ref, vlo, vhi)):
        x = src_ref[...].astype(F32)
        xs = pltpu.roll(x, SWA_HD, 1)
        zero = jnp.zeros_like(x)
        lo_ref[0] = jnp.where(low, x, zero).astype(BF16)
        hi_ref[0] = jnp.where(low, zero, xs).astype(BF16)
        lo_ref[1] = jnp.where(low, xs, zero).astype(BF16)
        hi_ref[1] = jnp.where(low, zero, x).astype(BF16)

    n_tiles = heads_per_step * SWA_HD // LANES
    tiles_per_kv = n_tiles // 2

    m_rows = tiles_per_kv * w
    row = lax.broadcasted_iota(jnp.int32, (m_rows, 1), 0)
    out_lane = lax.broadcasted_iota(jnp.int32, (m_rows, LANES), 1)

    def mask_bias(klen, is_first):
        qq = lax.broadcasted_iota(jnp.int32, (m_rows, klen), 0) % w
        kk = lax.broadcasted_iota(jnp.int32, (m_rows, klen), 1)
        valid = (kk <= qq) if is_first else ((kk > qq) & (kk <= qq + w))
        return jnp.where(valid, 0.0, NEG_INF).astype(F32)

    def sink_column(g, half):
        sink = jnp.zeros((m_rows, 1), F32)
        for ti in range(tiles_per_kv):
            head = pair * heads_per_step + 2 * (g * tiles_per_kv + ti) + half
            sink = jnp.where((row >= ti * w) & (row < (ti + 1) * w), sink_ref[head], sink)
        return sink

    sinks = [[sink_column(g, half) for half in range(2)] for g in range(2)]

    def block(r0, k0, klen, bias):
        for g in range(2):
            tiles = range(g * tiles_per_kv, (g + 1) * tiles_per_kv)
            q = jnp.concatenate([q_ref[pl.ds(r0, w), c * LANES:(c + 1) * LANES] for c in tiles], axis=0)
            kcat = jnp.concatenate([klo[g, pl.ds(k0, klen), :], khi[g, pl.ds(k0, klen), :]], axis=0)
            vcat = jnp.concatenate([vlo[g, pl.ds(k0, klen), :], vhi[g, pl.ds(k0, klen), :]], axis=0)
            s = _dot_nt(q, kcat)
            probs, inv_den = [], []
            for half in range(2):
                sink = sinks[g][half]
                sh = s[:, half * klen:(half + 1) * klen] + bias
                m = jnp.maximum(jnp.max(sh, axis=-1, keepdims=True), sink)
                p = jnp.exp(sh - m)
                inv_den.append(1.0 / (jnp.sum(p, axis=-1, keepdims=True) + jnp.exp(sink - m)))
                probs.append(p.astype(BF16))
            o = _dot(jnp.concatenate(probs, axis=1), vcat) * jnp.where(out_lane < SWA_HD, inv_den[0], inv_den[1])
            for ti, c in enumerate(tiles):
                o_ref[pl.ds(r0, w), c * LANES:(c + 1) * LANES] = o[ti * w:(ti + 1) * w].astype(o_ref.dtype)

    block(0, 0, w, mask_bias(w, True))
    band = mask_bias(2 * w, False)

    def body(n, carry):
        r0 = pl.multiple_of(n * w, w)
        block(r0, pl.multiple_of(r0 - w, w), 2 * w, band)
        return carry

    n_blk = seq // w
    lax.fori_loop(1, n_blk, body, 0, unroll=3 if (n_blk - 1) % 3 == 0 else 1)


def _swa_attention(z, sinks, d):
    t = d.tokens
    hps = 2 * (d.swa_heads // d.swa_kv_heads)
    qw = hps * SWA_HD
    n_pairs = _exact_div(d.swa_kv_heads, 2)
    jq, jk, jv = _exact_div(d.off_qs, qw), _exact_div(d.off_ks, LANES), _exact_div(d.off_vs, LANES)
    grid_spec = pltpu.PrefetchScalarGridSpec(
        num_scalar_prefetch=1,
        grid=(d.batch, n_pairs),
        in_specs=[
            pl.BlockSpec((d.seq, qw), lambda b, p, s: (b, jq + p)),
            pl.BlockSpec((d.seq, LANES), lambda b, p, s: (b, jk + p)),
            pl.BlockSpec((d.seq, LANES), lambda b, p, s: (b, jv + p)),
        ],
        out_specs=pl.BlockSpec((d.seq, qw), lambda b, p, s: (b, p)),
        scratch_shapes=[pltpu.VMEM((2, d.seq, LANES), BF16)] * 4,
    )
    return pl.pallas_call(
        functools.partial(_swa_kernel, seq=d.seq, window=d.window, heads_per_step=hps),
        out_shape=jax.ShapeDtypeStruct((t, d.swa_q), BF16),
        grid_spec=grid_spec,
        compiler_params=_cparams(("parallel", "parallel"), 48),
        name="swa_attention",
    )(sinks.astype(F32), z, z, z)


def _cross_router_kernel(h_ref, gc_ref, wq_ref, kvm_ref, wo_ref, gf_ref, wr_ref, br_ref,
                         h2_ref, xnf_ref, rt_ref, cnt_ref, run_ref, *, x_heads, n_groups, scale):
    tm = h_ref.shape[0]
    hx = x_heads * X_HD
    n_exp = n_groups * EPG

    @pl.when(pl.program_id(0) == 0)
    def _():
        run_ref[...] = jnp.zeros_like(run_ref)

    h = h_ref[...]
    hn = _rms(h, gc_ref[...]).astype(BF16)
    q = (_dot(hn, wq_ref[...]) * scale).astype(BF16)
    outs = []
    for hd in range(x_heads):
        kh = kvm_ref[:, hd * X_HD:(hd + 1) * X_HD]
        vh = kvm_ref[:, hx + hd * X_HD:hx + (hd + 1) * X_HD]
        s = _dot_nt(q[:, hd * X_HD:(hd + 1) * X_HD], kh)
        m = jnp.max(s, axis=-1, keepdims=True)
        p = jnp.exp(s - m)
        l = jnp.sum(p, axis=-1, keepdims=True)
        outs.append((_dot(p.astype(BF16), vh) / l).astype(BF16))
    h2 = h + _dot(jnp.concatenate(outs, axis=1), wo_ref[...])
    h2_ref[...] = h2
    xnf = _rms(h2, gf_ref[...])
    xnf_ref[...] = xnf

    logits = _dot(xnf.astype(BF16), wr_ref[...]) + br_ref[...]
    lane = lax.broadcasted_iota(jnp.int32, (tm, LANES), 1).astype(F32)
    big = float(LANES)
    is_group = (lane >= n_exp) & (lane < n_exp + n_groups)
    gl = jnp.where(is_group, logits, -jnp.inf)
    gmax = jnp.max(gl, axis=-1, keepdims=True)
    g_lane = jnp.min(jnp.where(gl == gmax, lane, big), axis=-1, keepdims=True)
    p_group = 1.0 / jnp.sum(jnp.where(is_group, jnp.exp(gl - gmax), 0.0), axis=-1, keepdims=True)
    e_lo = (g_lane - n_exp) * EPG
    in_group = (lane >= e_lo) & (lane < e_lo + EPG)
    el = jnp.where(in_group, logits, -jnp.inf)
    m1 = jnp.max(el, axis=-1, keepdims=True)
    i1 = jnp.min(jnp.where(el == m1, lane, big), axis=-1, keepdims=True)
    el2 = jnp.where(lane == i1, -jnp.inf, el)
    m2 = jnp.max(el2, axis=-1, keepdims=True)
    i2 = jnp.min(jnp.where(el2 == m2, lane, big), axis=-1, keepdims=True)
    w2 = jnp.exp(m2 - m1)
    gate1 = p_group / (1.0 + w2)
    gate2 = gate1 * w2

    hot1 = lane == i1
    hot2 = lane == i2
    onehot = jnp.where(hot1 | hot2, 1.0, 0.0)
    rr = lax.broadcasted_iota(jnp.int32, (tm, tm), 0)
    cc = lax.broadcasted_iota(jnp.int32, (tm, tm), 1)
    tri = jnp.where(cc < rr, 1.0, 0.0).astype(BF16)
    before = _dot(tri, onehot.astype(BF16)) + run_ref[...]
    rank1 = jnp.sum(jnp.where(hot1, before, 0.0), axis=-1, keepdims=True)
    rank2 = jnp.sum(jnp.where(hot2, before, 0.0), axis=-1, keepdims=True)
    run = run_ref[...] + jnp.sum(onehot, axis=0, keepdims=True)
    run_ref[...] = run
    cnt_ref[...] = run

    rt = jnp.where(lane == 0, i1, 0.0)
    rt = jnp.where(lane == 1, i2, rt)
    rt = jnp.where(lane == 2, gate1, rt)
    rt = jnp.where(lane == 3, gate2, rt)
    rt = jnp.where(lane == 4, rank1, rt)
    rt = jnp.where(lane == 5, rank2, rt)
    rt_ref[...] = rt


def _cross_router(h, g_cross, w_xq, kvm, w_xo, g_ffn, w_r, b_r, d):
    t = d.tokens
    tm = d.t_cross
    dm = d.d_model
    hx = d.x_heads * X_HD
    steps_per_batch = _exact_div(d.seq, tm)
    row = lambda i: (i, 0)
    fixed = lambda i: (0, 0)
    return pl.pallas_call(
        functools.partial(_cross_router_kernel, x_heads=d.x_heads, n_groups=d.n_groups, scale=X_HD ** -0.5),
        out_shape=(
            jax.ShapeDtypeStruct((t, dm), F32),
            jax.ShapeDtypeStruct((t, dm), F32),
            jax.ShapeDtypeStruct((t, LANES), F32),
            jax.ShapeDtypeStruct((1, LANES), F32),
        ),
        grid=(_exact_div(t, tm),),
        in_specs=[
            pl.BlockSpec((tm, dm), row),
            pl.BlockSpec((1, dm), fixed),
            pl.BlockSpec((dm, hx), fixed),
            pl.BlockSpec((d.mem_len, 2 * hx), lambda i: (i // steps_per_batch, 0)),
            pl.BlockSpec((hx, dm), fixed),
            pl.BlockSpec((1, dm), fixed),
            pl.BlockSpec((dm, LANES), fixed),
            pl.BlockSpec((1, LANES), fixed),
        ],
        out_specs=(
            pl.BlockSpec((tm, dm), row),
            pl.BlockSpec((tm, dm), row),
            pl.BlockSpec((tm, LANES), row),
            pl.BlockSpec((1, LANES), fixed),
        ),
        scratch_shapes=[pltpu.VMEM((1, LANES), F32)],
        compiler_params=_cparams(("arbitrary",), 56),
        name="cross_attention_router",
    )(h, g_cross.reshape(1, dm), w_xq, kvm, w_xo, g_ffn.reshape(1, dm), w_r, b_r)


def _row_copy(src_hbm, src_row, dst_ref, dst_row, sem):
    return pltpu.make_async_copy(src_hbm.at[pl.ds(src_row, 1)], dst_ref.at[pl.ds(dst_row, 1)], sem)


def _dispatch_kernel(dest_ref, fill_ref, x_ref, xs_hbm, zero_ref, sem, zrow_sem, zblk_sem, *,
                     t_tok, n_experts, moe_block, n_blocks):
    base = pl.program_id(0) * t_tok

    @pl.when(pl.program_id(0) == 0)
    def _():
        zero_ref[...] = jnp.zeros_like(zero_ref)
        zero_block = lambda b: pltpu.make_async_copy(
            zero_ref, xs_hbm.at[pl.ds(pl.multiple_of(b * moe_block, moe_block), moe_block)], zblk_sem)

        def for_fill_rows(fn):
            for e in range(n_experts):
                lax.fori_loop(fill_ref[2 * e], fill_ref[2 * e + 1], fn, 0)

        def start_row(r, carry):
            _row_copy(zero_ref, 0, xs_hbm, r, zrow_sem).start()
            return carry

        def wait_row(r, carry):
            _row_copy(zero_ref, 0, xs_hbm, 0, zrow_sem).wait()
            return carry

        def start_block(b, carry):
            zero_block(b).start()
            return carry

        def wait_block(b, carry):
            zero_block(0).wait()
            return carry

        for_fill_rows(start_row)
        lax.fori_loop(fill_ref[2 * n_experts], n_blocks, start_block, 0)
        for_fill_rows(wait_row)
        lax.fori_loop(fill_ref[2 * n_experts], n_blocks, wait_block, 0)

    def issue(r, carry):
        for k in range(TOP_K):
            _row_copy(x_ref, r, xs_hbm, dest_ref[TOP_K * (base + r) + k], sem).start()
        return carry

    def drain(r, carry):
        for k in range(TOP_K):
            _row_copy(x_ref, 0, xs_hbm, 0, sem).wait()
        return carry

    lax.fori_loop(0, t_tok, issue, 0)
    lax.fori_loop(0, t_tok, drain, 0)


def _dispatch(dest, fill, x, d):
    t = d.tokens
    p_rows = d.n_blocks * d.moe_block
    grid_spec = pltpu.PrefetchScalarGridSpec(
        num_scalar_prefetch=2,
        grid=(_exact_div(t, d.t_tok),),
        in_specs=[pl.BlockSpec((d.t_tok, d.d_model), lambda i, dest, fill: (i, 0))],
        out_specs=pl.BlockSpec(memory_space=pl.ANY),
        scratch_shapes=[pltpu.VMEM((d.moe_block, d.d_model), x.dtype)] + [pltpu.SemaphoreType.DMA(())] * 3,
    )
    return pl.pallas_call(
        functools.partial(_dispatch_kernel, t_tok=d.t_tok, n_experts=d.n_experts, moe_block=d.moe_block,
                          n_blocks=d.n_blocks),
        out_shape=jax.ShapeDtypeStruct((p_rows, d.d_model), x.dtype),
        grid_spec=grid_spec,
        compiler_params=_cparams(("arbitrary",), 32),
        name="moe_dispatch",
    )(dest, fill, x)


W_SLABS = 4


def _moe_steps(d):
    return W_SLABS + d.n_blocks + W_SLABS * d.n_experts + 1


def _moe_schedule(padded, d):
    bm, ne, nblk = d.moe_block, d.n_experts, d.n_blocks
    i32 = jnp.int32
    nb = padded // bm
    first_blk = (jnp.cumsum(padded) - padded) // bm
    n_used = jnp.sum(nb)
    per_expert = jnp.where(jnp.arange(ne) == ne - 1, nb, jnp.maximum(nb, W_SLABS))
    step_end = W_SLABS + jnp.cumsum(per_expert)
    step_start = step_end - per_expert
    total = step_end[-1]
    s = jnp.arange(_moe_steps(d), dtype=i32)
    is_pro = s < W_SLABS
    is_tail = s >= total
    e = jnp.minimum(jnp.sum((step_end[None, :] <= s[:, None]).astype(i32), axis=1), ne - 1)
    j = jnp.where(is_pro, s, s - step_start[e])
    has_blk = jnp.logical_not(is_pro | is_tail) & (j < nb[e])
    blk = jnp.where(is_pro, 0, first_blk[e] + jnp.minimum(j, nb[e]))
    xblk = jnp.where(has_blk, blk, 0)
    oblk = jnp.where(is_tail, jnp.minimum(n_used + s - total, nblk), blk)
    nxt = jnp.where(is_pro, 0, jnp.minimum(e + 1, ne - 1))
    slab = jnp.where(is_tail, W_SLABS - 1, jnp.minimum(j, W_SLABS - 1))
    slot = jnp.where(is_pro, 1, e % 2)
    return tuple(v.astype(i32) for v in (xblk, oblk, nxt, slab, slot, is_tail))


def _silu_mul(a, b):
    return (a / (1.0 + jnp.exp(-a))) * b


def _expert_up_kernel(xblk, oblk, nxt, slab, slot, tail, x_ref, wg_st, wu_st, hb_ref, wg0, wg1, wu0, wu1):
    del xblk, oblk, nxt
    s = pl.program_id(0)
    rows = wg_st.shape[1]

    @pl.when(s == 0)
    def _():
        wg1[...] = jnp.zeros_like(wg1)
        wu1[...] = jnp.zeros_like(wu1)

    @pl.when(tail[s] == 1)
    def _():
        hb_ref[...] = jnp.zeros_like(hb_ref)

    r0 = pl.multiple_of(slab[s] * rows, rows)
    for cur, (wg_c, wu_c, wg_n, wu_n) in enumerate(((wg0, wu0, wg1, wu1), (wg1, wu1, wg0, wu0))):
        @pl.when((tail[s] == 0) & (slot[s] == cur))
        def _():
            wg_n[pl.ds(r0, rows), :] = wg_st[0].astype(BF16)
            wu_n[pl.ds(r0, rows), :] = wu_st[0].astype(BF16)
            x = x_ref[...].astype(BF16)
            hb_ref[...] = _silu_mul(_dot(x, wg_c[...]), _dot(x, wu_c[...])).astype(hb_ref.dtype)


def _expert_down_kernel(xblk, oblk, nxt, slab, slot, tail, hb_ref, wd_st, y_ref, wd0, wd1):
    del xblk, oblk, nxt
    s = pl.program_id(0)
    rows = wd_st.shape[1]

    @pl.when(s == 0)
    def _():
        wd1[...] = jnp.zeros_like(wd1)

    @pl.when(tail[s] == 1)
    def _():
        y_ref[...] = jnp.zeros_like(y_ref)

    r0 = pl.multiple_of(slab[s] * rows, rows)
    for cur, (wd_c, wd_n) in enumerate(((wd0, wd1), (wd1, wd0))):
        @pl.when((tail[s] == 0) & (slot[s] == cur))
        def _():
            wd_n[pl.ds(r0, rows), :] = wd_st[0].astype(BF16)
            y_ref[...] = _dot(hb_ref[...], wd_c[...])


def _experts(sched, xs, w_gate, w_up, w_down, layer, d):
    bm = d.moe_block
    dm, de = d.d_model, d.d_expert
    ne = d.n_experts
    rows_out = (d.n_blocks + 1) * bm
    w_gate, w_up = w_gate.reshape(-1, dm, de), w_up.reshape(-1, dm, de)
    w_down = w_down.reshape(-1, de, dm)
    x_map = lambda s, xb, ob, nx, sl, st, tl: (xb[s], 0)
    o_map = lambda s, xb, ob, nx, sl, st, tl: (ob[s], 0)
    w_map = lambda s, xb, ob, nx, sl, st, tl: (layer * ne + nx[s], sl[s], 0)
    ku, kd = _exact_div(dm, W_SLABS), _exact_div(de, W_SLABS)
    hb = pl.pallas_call(
        _expert_up_kernel,
        out_shape=jax.ShapeDtypeStruct((rows_out, de), BF16),
        grid_spec=pltpu.PrefetchScalarGridSpec(
            num_scalar_prefetch=len(sched),
            grid=(_moe_steps(d),),
            in_specs=[pl.BlockSpec((bm, dm), x_map), pl.BlockSpec((1, ku, de), w_map),
                      pl.BlockSpec((1, ku, de), w_map)],
            out_specs=pl.BlockSpec((bm, de), o_map),
            scratch_shapes=[pltpu.VMEM((dm, de), BF16)] * 4,
        ),
        compiler_params=_cparams(("arbitrary",), 56),
        name="moe_experts_up",
    )(*sched, xs, w_gate, w_up)
    return pl.pallas_call(
        _expert_down_kernel,
        out_shape=jax.ShapeDtypeStruct((rows_out, dm), F32),
        grid_spec=pltpu.PrefetchScalarGridSpec(
            num_scalar_prefetch=len(sched),
            grid=(_moe_steps(d),),
            in_specs=[pl.BlockSpec((bm, de), x_map), pl.BlockSpec((1, kd, dm), w_map)],
            out_specs=pl.BlockSpec((bm, dm), o_map),
            scratch_shapes=[pltpu.VMEM((de, dm), BF16)] * 2,
        ),
        compiler_params=_cparams(("arbitrary",), 40),
        name="moe_experts_down",
    )(*sched, hb, w_down)


def _combine_kernel(dest_ref, y_hbm, h_ref, rt_ref, g_ref, *rest, t_tok, last):
    if last:
        out_ref, ybuf, sem = rest
    else:
        h3_ref, out_ref, ybuf, sem = rest
    i = pl.program_id(0)
    buf = i % 2

    def gather(step, b):
        base = step * t_tok

        def issue(r, carry):
            for k in range(TOP_K):
                _row_copy(y_hbm, dest_ref[TOP_K * (base + r) + k], ybuf.at[b, k], r, sem.at[b]).start()
            return carry

        lax.fori_loop(0, t_tok, issue, 0)

    @pl.when(i == 0)
    def _():
        gather(0, 0)

    @pl.when(i + 1 < pl.num_programs(0))
    def _():
        gather(i + 1, 1 - buf)

    def drain(r, carry):
        for k in range(TOP_K):
            _row_copy(y_hbm, 0, ybuf.at[buf, k], 0, sem.at[buf]).wait()
        return carry

    lax.fori_loop(0, t_tok, drain, 0)
    rt = rt_ref[...]
    h3 = h_ref[...] + rt[:, 2:3] * ybuf[buf, 0] + rt[:, 3:4] * ybuf[buf, 1]
    if not last:
        h3_ref[...] = h3
    out_ref[...] = _rms(h3, g_ref[...]).astype(out_ref.dtype)


def _combine(dest, y, h2, rt, g_next, d, last):
    t = d.tokens
    tt = d.t_tok
    dm = d.d_model
    row = lambda i, ds: (i, 0)
    out_specs = pl.BlockSpec((tt, dm), row)
    if last:
        out_shape = jax.ShapeDtypeStruct((t, dm), F32)
    else:
        out_shape = (jax.ShapeDtypeStruct((t, dm), F32), jax.ShapeDtypeStruct((t, dm), BF16))
        out_specs = (out_specs, pl.BlockSpec((tt, dm), row))
    grid_spec = pltpu.PrefetchScalarGridSpec(
        num_scalar_prefetch=1,
        grid=(_exact_div(t, tt),),
        in_specs=[
            pl.BlockSpec(memory_space=pl.ANY),
            pl.BlockSpec((tt, dm), row),
            pl.BlockSpec((tt, LANES), row),
            pl.BlockSpec((1, dm), lambda i, ds: (0, 0)),
        ],
        out_specs=out_specs,
        scratch_shapes=[pltpu.VMEM((2, TOP_K, tt, dm), F32), pltpu.SemaphoreType.DMA((2,))],
    )
    return pl.pallas_call(
        functools.partial(_combine_kernel, t_tok=tt, last=last),
        out_shape=out_shape,
        grid_spec=grid_spec,
        compiler_params=_cparams(("arbitrary",), 48),
        name="moe_combine",
    )(dest, y, h2, rt, g_next.reshape(1, dm))


def _layer(h, xn, cos, sin, memn, p, d, g_next, last):
    t = d.tokens
    dm = d.d_model
    tm = d.tm
    row128 = pl.BlockSpec((tm, LANES), lambda i, j: (i, 0))
    row128_in = pl.BlockSpec((d.tm_in, LANES), lambda i, j: (i, 0))

    tn = d.tn_in
    z = _matmul(
        xn, p["w_main"], k=dm, a_col=0, tm=d.tm_in, tn=tn, out_dtype=BF16, vmem_mib=60,
        post=functools.partial(_post_in_proj, j_rope0=_exact_div(d.off_qs, tn), j_k=_exact_div(d.off_ks, tn),
                               j_rope1=_exact_div(d.off_vs, tn), tn=tn, q_scale=SWA_HD ** -0.5),
        extras=((cos, row128_in), (sin, row128_in)), name="in_proj")
    kpe = _matmul(xn, p["w_kr"], k=dm, a_col=0, tm=tm, tn=LANES, out_dtype=BF16, post=_post_rope_all,
                  extras=((cos, row128), (sin, row128)), name="rope_key_proj")

    tn_q = min(4, d.mla_heads) * MLA_HEAD_PAD
    q_full = _matmul(
        z, p["w_uq"], k=d.q_lora, a_col=0, tm=tm, tn=tn_q, out_dtype=BF16, pre=_pre_rms,
        post=functools.partial(_post_mla_q, scale=(NOPE_DIM + ROPE_DIM) ** -0.5),
        extras=((p["g_qa"], pl.BlockSpec((1, d.q_lora), lambda i, j: (0, 0))), (cos, row128), (sin, row128)),
        name="mla_q_proj")
    kv = _matmul(
        z, p["w_ukv"], k=d.kv_lora, a_col=_exact_div(d.off_ckv, d.kv_lora), tm=tm, tn=tn_q, out_dtype=BF16,
        pre=_pre_rms, post=_post_cast,
        extras=((p["g_kva"], pl.BlockSpec((1, d.kv_lora), lambda i, j: (0, 0))),), name="mla_kv_proj")
    o_a = _mla_attention(q_full, kv, kpe, d)

    o_b = _swa_attention(z, p["sinks"], d)

    merged = _merge(o_a, p["w_pa"], o_b, p["w_pb"], z, d)
    h1 = _matmul(merged, p["w_o"], k=dm, a_col=0, tm=tm, tn=d.tn, out_dtype=F32, post=_post_residual,
                 extras=((h, pl.BlockSpec((tm, d.tn), lambda i, j: (i, j))),), name="out_proj")

    kvm = _matmul(memn, p["w_xkv"], k=dm, a_col=0, tm=memn.shape[0], tn=d.tn, out_dtype=BF16,
                  post=_post_cast, name="mem_kv_proj")
    h2, xnf, rt, cnt = _cross_router(h1, p["g_cross"], p["w_xq"], kvm, p["w_xo"], p["g_ffn"],
                                     p["w_r"], p["b_r"], d)

    bm = d.moe_block
    counts = cnt[0, :d.n_experts].astype(jnp.int32)
    padded = (counts + bm - 1) // bm * bm
    pad_ends = jnp.cumsum(padded)
    pad_starts = pad_ends - padded
    expert = rt[:, 0:TOP_K].astype(jnp.int32)
    dest = (pad_starts[expert] + rt[:, 4:4 + TOP_K].astype(jnp.int32)).reshape(t * TOP_K)
    fill = jnp.concatenate([jnp.stack([pad_starts + counts, pad_ends], axis=1).reshape(-1),
                            pad_ends[-1:] // bm]).astype(jnp.int32)

    xs = _dispatch(dest, fill, xnf, d)
    y = _experts(_moe_schedule(padded, d), xs, p["w_gate"], p["w_up"], p["w_down"], p["layer"], d)
    return _combine(dest, y, h2, rt, g_next, d, last)


def _prep_layer(l, d, g_mix, w_in, g_qa, g_kva, w_uq, w_ukv, sinks, w_pa, w_pb, w_o, g_cross, w_xq, w_xkv,
                w_xo, g_ffn, w_group, b_group, w_router, b_router, w_gate, w_up, w_down):
    dm = d.d_model
    kr0 = d.q_lora + d.kv_lora
    w = w_in[l]
    w_main = jnp.concatenate([w[:, :kr0], w[:, kr0 + ROPE_DIM:]], axis=1).astype(BF16)
    w_kr = jnp.pad(w[:, kr0:kr0 + ROPE_DIM], ((0, 0), (0, LANES - ROPE_DIM))).astype(BF16)
    qk = NOPE_DIM + ROPE_DIM
    wq = w_uq[l].reshape(d.q_lora, d.mla_heads, qk)
    wq = jnp.pad(wq, ((0, 0), (0, 0), (0, MLA_HEAD_PAD - qk))).reshape(d.q_lora, d.mla_heads * MLA_HEAD_PAD)
    n_r = d.n_experts + d.n_groups
    w_r = jnp.pad(jnp.concatenate([w_router[l], w_group[l]], axis=1), ((0, 0), (0, LANES - n_r)))
    b_r = jnp.pad(jnp.concatenate([b_router[l], b_group[l]]), (0, LANES - n_r)).reshape(1, LANES)
    return dict(
        w_main=w_main, w_kr=w_kr, w_uq=wq.astype(BF16), w_ukv=w_ukv[l].astype(BF16),
        g_qa=g_qa[l].reshape(1, -1).astype(F32), g_kva=g_kva[l].reshape(1, -1).astype(F32),
        sinks=sinks[l], w_pa=w_pa[l].astype(BF16), w_pb=w_pb[l].astype(BF16), w_o=w_o[l].astype(BF16),
        g_cross=g_cross[l].astype(F32), w_xq=w_xq[l].astype(BF16), w_xkv=w_xkv[l].astype(BF16),
        w_xo=w_xo[l].astype(BF16), g_ffn=g_ffn[l].astype(F32), w_r=w_r.astype(BF16), b_r=b_r.astype(F32),
        w_gate=w_gate, w_up=w_up, w_down=w_down, layer=l,
    )


def _forward(d, x, mem, positions, g_mix, w_in, g_qa, g_kva, w_uq, w_ukv, sinks, w_pa, w_pb, w_o,
             g_cross, g_mem, w_xq, w_xkv, w_xo, g_ffn, w_group, b_group, w_router, b_router,
             w_gate, w_up, w_down, g_final):
    depth = w_in.shape[0]
    t = d.tokens
    dm = d.d_model
    cos, sin = _rope_tables(positions, d)
    memn = _rmsnorm(mem.reshape(d.batch * d.mem_len, dm), g_mem, BF16, d.t_norm)
    h = x.reshape(t, dm)
    xn = _rmsnorm(h, g_mix[0], BF16, d.t_norm)
    for l in range(depth):
        p = _prep_layer(l, d, g_mix, w_in, g_qa, g_kva, w_uq, w_ukv, sinks, w_pa, w_pb, w_o, g_cross,
                        w_xq, w_xkv, w_xo, g_ffn, w_group, b_group, w_router, b_router, w_gate, w_up, w_down)
        last = l == depth - 1
        g_next = g_final if last else g_mix[l + 1]
        res = _layer(h, xn, cos, sin, memn, p, d, g_next, last)
        if last:
            return res.reshape(d.batch, d.seq, dm)
        h, xn = res


def kernel(x, mem, positions, g_mix, w_in, g_qa, g_kva, w_uq, w_ukv, sinks, w_pa, w_pb, w_o, g_cross, g_mem,
           w_xq, w_xkv, w_xo, g_ffn, w_group, b_group, w_router, b_router, w_gate, w_up, w_down, g_final):
    return _forward(Dims(), x, mem, positions, g_mix, w_in, g_qa, g_kva, w_uq, w_ukv, sinks, w_pa, w_pb, w_o,
                    g_cross, g_mem, w_xq, w_xkv, w_xo, g_ffn, w_group, b_group, w_router, b_router,
                    w_gate, w_up, w_down, g_final)
```

```python
import functools
from typing import NamedTuple

import jax
import jax.numpy as jnp
from jax import lax
from jax.experimental import pallas as pl
from jax.experimental.pallas import tpu as pltpu

F32 = jnp.float32
BF16 = jnp.bfloat16
EPS = 1e-6
ROPE_THETA = 10000.0
NEG_INF = -1e30
LANES = 128
ROPE_DIM = 64
NOPE_DIM = 128
MLA_V_DIM = 128
MLA_HEAD_PAD = 256
SWA_HD = 64
X_HD = 128
EPG = 8
TOP_K = 2
MIB = 1024 * 1024


class Dims(NamedTuple):
    batch: int = 4
    seq: int = 2048
    d_model: int = 4096
    mem_len: int = 256
    mla_heads: int = 16
    q_lora: int = 1024
    kv_lora: int = 512
    swa_heads: int = 32
    swa_kv_heads: int = 8
    window: int = 128
    x_heads: int = 4
    n_groups: int = 4
    d_expert: int = 768
    moe_block: int = 128
    tm: int = 1024
    tn_in: int = 512
    tn: int = 512
    tq: int = 512
    t_cross: int = 256
    t_tok: int = 256
    t_norm: int = 256

    @property
    def tokens(self):
        return self.batch * self.seq

    @property
    def n_experts(self):
        return self.n_groups * EPG

    @property
    def swa_q(self):
        return self.swa_heads * SWA_HD

    @property
    def swa_kv(self):
        return self.swa_kv_heads * SWA_HD

    @property
    def off_ckv(self):
        return self.q_lora

    @property
    def off_qs(self):
        return self.q_lora + self.kv_lora

    @property
    def off_ks(self):
        return self.off_qs + self.swa_q

    @property
    def off_vs(self):
        return self.off_ks + self.swa_kv

    @property
    def off_ga(self):
        return self.off_vs + self.swa_kv

    @property
    def off_gb(self):
        return self.off_ga + self.d_model

    @property
    def n_main(self):
        return self.off_gb + self.d_model

    @property
    def n_blocks(self):
        return -(-(self.tokens * TOP_K) // self.moe_block) + self.n_experts


def _exact_div(a, b):
    assert a % b == 0, (a, b)
    return a // b


def _cparams(sem, vmem_mib):
    return pltpu.CompilerParams(dimension_semantics=sem, vmem_limit_bytes=vmem_mib * MIB)


def _rms(x, g):
    return x * lax.rsqrt(jnp.mean(x * x, axis=-1, keepdims=True) + EPS) * g


def _rope128(x, cos, sin):
    lane = lax.broadcasted_iota(jnp.int32, x.shape, 1)
    first_half = (lane % ROPE_DIM) < (ROPE_DIM // 2)
    rot = jnp.where(first_half, -pltpu.roll(x, LANES - ROPE_DIM // 2, 1), pltpu.roll(x, ROPE_DIM // 2, 1))
    return x * cos + rot * sin


def _dot(a, b):
    return jnp.dot(a, b, preferred_element_type=F32)


def _dot_nt(a, b):
    return lax.dot_general(a, b, (((1,), (1,)), ((), ())), preferred_element_type=F32)


def _rope_table_kernel(pos_ref, inv_ref, cos_ref, sin_ref):
    ang = pos_ref[...].astype(F32) * inv_ref[...]
    cos_ref[...] = jnp.cos(ang)
    sin_ref[...] = jnp.sin(ang)


def _rope_tables(positions, d):
    t = d.tokens
    half = ROPE_DIM // 2
    inv_freq = 1.0 / (ROPE_THETA ** (jnp.arange(0, ROPE_DIM, 2, dtype=F32) / ROPE_DIM))
    inv = jnp.tile(inv_freq, LANES // half).reshape(1, LANES)
    tb = min(t, 1024)
    return pl.pallas_call(
        _rope_table_kernel,
        out_shape=(jax.ShapeDtypeStruct((t, LANES), F32),) * 2,
        grid=(_exact_div(t, tb),),
        in_specs=[pl.BlockSpec((tb, 1), lambda i: (i, 0)), pl.BlockSpec((1, LANES), lambda i: (0, 0))],
        out_specs=(pl.BlockSpec((tb, LANES), lambda i: (i, 0)),) * 2,
        name="rope_tables",
    )(positions.reshape(t, 1), inv)


def _rmsnorm_kernel(x_ref, g_ref, o_ref):
    o_ref[...] = _rms(x_ref[...].astype(F32), g_ref[...]).astype(o_ref.dtype)


def _rmsnorm(x, g, out_dtype, tm):
    m, dd = x.shape
    return pl.pallas_call(
        _rmsnorm_kernel,
        out_shape=jax.ShapeDtypeStruct((m, dd), out_dtype),
        grid=(_exact_div(m, tm),),
        in_specs=[pl.BlockSpec((tm, dd), lambda i: (i, 0)), pl.BlockSpec((1, dd), lambda i: (0, 0))],
        out_specs=pl.BlockSpec((tm, dd), lambda i: (i, 0)),
        compiler_params=_cparams(("parallel",), 32),
        name="rmsnorm",
    )(x, g.reshape(1, dd).astype(F32))


def _mm_kernel(*refs, pre, post, n_extra):
    a_ref, w_ref = refs[0], refs[1]
    extras = refs[2:2 + n_extra]
    out_ref = refs[2 + n_extra]
    a = a_ref[...]
    if pre is not None:
        a = pre(a, extras)
    post(_dot(a, w_ref[...]), extras, out_ref)


def _matmul(a, w, *, k, a_col, tm, tn, out_dtype, post, pre=None, extras=(), vmem_mib=48, name):
    m = a.shape[0]
    n = w.shape[1]
    assert w.shape[0] == k
    in_specs = [pl.BlockSpec((tm, k), lambda i, j: (i, a_col)), pl.BlockSpec((k, tn), lambda i, j: (0, j))]
    in_specs += [s for _, s in extras]
    return pl.pallas_call(
        functools.partial(_mm_kernel, pre=pre, post=post, n_extra=len(extras)),
        out_shape=jax.ShapeDtypeStruct((m, n), out_dtype),
        grid=(_exact_div(m, tm), _exact_div(n, tn)),
        in_specs=in_specs,
        out_specs=pl.BlockSpec((tm, tn), lambda i, j: (i, j)),
        compiler_params=_cparams(("parallel", "arbitrary"), vmem_mib),
        name=name,
    )(a, w, *[x for x, _ in extras])


def _post_cast(acc, extras, o_ref):
    o_ref[...] = acc.astype(o_ref.dtype)


def _pre_rms(a, extras):
    return _rms(a.astype(F32), extras[0][...]).astype(BF16)


def _post_residual(acc, extras, o_ref):
    o_ref[...] = extras[0][...] + acc


def _in_proj_kernel(x_ref, wa_ref, wb_ref, cos_ref, sin_ref, o_ref, w_scr, *, j_shift, **post_args):
    j = pl.program_id(0)
    first_row_tile = pl.program_id(1) == 0
    tn = w_scr.shape[1]

    @pl.when(first_row_tile & (j < j_shift))
    def _():
        w_scr[...] = wa_ref[0].astype(BF16)

    @pl.when(first_row_tile & (j >= j_shift))
    def _():
        n_ch = tn // LANES
        chunks = [wa_ref[0, :, c * LANES:(c + 1) * LANES] for c in range(n_ch)] + [wb_ref[0]]
        rolled = [pltpu.roll(ch, LANES - ROPE_DIM, 1) for ch in chunks]
        lane = lax.broadcasted_iota(jnp.int32, rolled[0].shape, 1)
        for c in range(n_ch):
            w_scr[:, c * LANES:(c + 1) * LANES] = jnp.where(
                lane < LANES - ROPE_DIM, rolled[c], rolled[c + 1]).astype(BF16)

    _post_in_proj(_dot(x_ref[...], w_scr[...]), (cos_ref, sin_ref), o_ref, j=j, tn=tn, **post_args)


def _in_proj(xn, w_in, layer, cos, sin, d):
    t, dm = xn.shape
    tm, tn = d.tm, d.tn_in
    kr0 = d.q_lora + d.kv_lora
    ch = tn // LANES
    return pl.pallas_call(
        functools.partial(
            _in_proj_kernel, j_shift=_exact_div(kr0, tn), j_rope0=_exact_div(d.off_qs, tn),
            j_k=_exact_div(d.off_ks, tn), j_rope1=_exact_div(d.off_vs, tn), q_scale=SWA_HD ** -0.5),
        out_shape=jax.ShapeDtypeStruct((t, d.n_main), BF16),
        grid=(_exact_div(d.n_main, tn), _exact_div(t, tm)),
        in_specs=[
            pl.BlockSpec((tm, dm), lambda j, i: (i, 0)),
            pl.BlockSpec((1, dm, tn), lambda j, i: (layer, 0, j)),
            pl.BlockSpec((1, dm, LANES), lambda j, i: (layer, 0, ch * (j + 1))),
            pl.BlockSpec((tm, LANES), lambda j, i: (i, 0)),
            pl.BlockSpec((tm, LANES), lambda j, i: (i, 0)),
        ],
        out_specs=pl.BlockSpec((tm, tn), lambda j, i: (i, j)),
        scratch_shapes=[pltpu.VMEM((dm, tn), BF16)],
        compiler_params=_cparams(("arbitrary", "arbitrary"), 56),
        name="in_proj",
    )(xn, w_in, w_in, cos, sin)


def _post_in_proj(acc, extras, o_ref, *, j, j_rope0, j_k, j_rope1, tn, q_scale):
    cos_ref, sin_ref = extras
    is_rope = (j >= j_rope0) & (j < j_rope1)

    @pl.when(is_rope)
    def _():
        scale = jnp.where(j < j_k, q_scale, 1.0).astype(F32)
        cos = cos_ref[...]
        sin = sin_ref[...]
        for c in range(tn // LANES):
            sl = slice(c * LANES, (c + 1) * LANES)
            o_ref[:, sl] = (_rope128(acc[:, sl], cos, sin) * scale).astype(o_ref.dtype)

    @pl.when(jnp.logical_not(is_rope))
    def _():
        o_ref[...] = acc.astype(o_ref.dtype)


def _post_rope_all(acc, extras, o_ref):
    cos_ref, sin_ref = extras
    cos = cos_ref[...]
    sin = sin_ref[...]
    for c in range(acc.shape[1] // LANES):
        sl = slice(c * LANES, (c + 1) * LANES)
        o_ref[:, sl] = _rope128(acc[:, sl], cos, sin).astype(o_ref.dtype)


def _post_mla_q(acc, extras, o_ref, *, scale):
    _, cos_ref, sin_ref = extras
    cos = cos_ref[...]
    sin = sin_ref[...]
    for hd in range(acc.shape[1] // MLA_HEAD_PAD):
        lo = slice(hd * MLA_HEAD_PAD, hd * MLA_HEAD_PAD + LANES)
        hi = slice(hd * MLA_HEAD_PAD + LANES, (hd + 1) * MLA_HEAD_PAD)
        o_ref[:, lo] = (acc[:, lo] * scale).astype(o_ref.dtype)
        o_ref[:, hi] = (_rope128(acc[:, hi], cos, sin) * scale).astype(o_ref.dtype)


def _merge_kernel(oa_ref, wpa_ref, ob_ref, wpb_ref, ga_ref, gb_ref, o_ref):
    pa = _dot(oa_ref[...], wpa_ref[...])
    pb = _dot(ob_ref[...], wpb_ref[...])
    sa = 1.0 / (1.0 + jnp.exp(-ga_ref[...].astype(F32)))
    sb = 1.0 / (1.0 + jnp.exp(-gb_ref[...].astype(F32)))
    o_ref[...] = (sa * pa + sb * pb).astype(o_ref.dtype)


def _merge(o_a, w_pa, o_b, w_pb, z, d):
    t = d.tokens
    tm, tn = d.tm, d.tn
    ka, kb = o_a.shape[1], o_b.shape[1]
    ja, jb = _exact_div(d.off_ga, tn), _exact_div(d.off_gb, tn)
    return pl.pallas_call(
        _merge_kernel,
        out_shape=jax.ShapeDtypeStruct((t, d.d_model), BF16),
        grid=(_exact_div(t, tm), _exact_div(d.d_model, tn)),
        in_specs=[
            pl.BlockSpec((tm, ka), lambda i, j: (i, 0)),
            pl.BlockSpec((ka, tn), lambda i, j: (0, j)),
            pl.BlockSpec((tm, kb), lambda i, j: (i, 0)),
            pl.BlockSpec((kb, tn), lambda i, j: (0, j)),
            pl.BlockSpec((tm, tn), lambda i, j: (i, j + ja)),
            pl.BlockSpec((tm, tn), lambda i, j: (i, j + jb)),
        ],
        out_specs=pl.BlockSpec((tm, tn), lambda i, j: (i, j)),
        compiler_params=_cparams(("parallel", "arbitrary"), 48),
        name="gated_merge",
    )(o_a, w_pa, o_b, w_pb, z, z)


def _mla_kernel(q_ref, kv_ref, kpe_ref, o_ref, k_scr, *, seq, tq):
    k_scr[:, :NOPE_DIM] = kv_ref[:, :NOPE_DIM]
    k_scr[:, NOPE_DIM:] = kpe_ref[...]
    for i in range(seq // tq):
        ln = (i + 1) * tq
        q = q_ref[i * tq:(i + 1) * tq, :]
        s = _dot_nt(q, k_scr[:ln, :])
        row = lax.broadcasted_iota(jnp.int32, (tq, ln), 0) + i * tq
        col = lax.broadcasted_iota(jnp.int32, (tq, ln), 1)
        s = jnp.where(col <= row, s, NEG_INF)
        m = jnp.max(s, axis=-1, keepdims=True)
        p = jnp.exp(s - m)
        l = jnp.sum(p, axis=-1, keepdims=True)
        o = _dot(p.astype(BF16), kv_ref[:ln, NOPE_DIM:])
        o_ref[i * tq:(i + 1) * tq, :] = (o / l).astype(o_ref.dtype)


def _mla_attention(q_full, kv, kpe, d):
    t = d.tokens
    return pl.pallas_call(
        functools.partial(_mla_kernel, seq=d.seq, tq=d.tq),
        out_shape=jax.ShapeDtypeStruct((t, d.mla_heads * MLA_V_DIM), BF16),
        grid=(d.batch, d.mla_heads),
        in_specs=[
            pl.BlockSpec((d.seq, MLA_HEAD_PAD), lambda b, h: (b, h)),
            pl.BlockSpec((d.seq, NOPE_DIM + MLA_V_DIM), lambda b, h: (b, h)),
            pl.BlockSpec((d.seq, LANES), lambda b, h: (b, 0)),
        ],
        out_specs=pl.BlockSpec((d.seq, MLA_V_DIM), lambda b, h: (b, h)),
        scratch_shapes=[pltpu.VMEM((d.seq, MLA_HEAD_PAD), BF16)],
        compiler_params=_cparams(("parallel", "parallel"), 48),
        name="mla_attention",
    )(q_full, kv, kpe)


def _swa_kernel(sink_ref, q_ref, k_ref, v_ref, o_ref, klo, khi, vlo, vhi, *, seq, window, heads_per_step):
    pair = pl.program_id(1)
    w = window
    lane = lax.broadcasted_iota(jnp.int32, (seq, LANES), 1)
    low = lane < SWA_HD
    for src_ref, lo_ref, hi_ref in ((k_ref, klo, khi), (v_ref, vlo, vhi)):
        x = src_ref[...].astype(F32)
        xs = pltpu.roll(x, SWA_HD, 1)
        zero = jnp.zeros_like(x)
        lo_ref[0] = jnp.where(low, x, zero).astype(BF16)
        hi_ref[0] = jnp.where(low, zero, xs).astype(BF16)
        lo_ref[1] = jnp.where(low, xs, zero).astype(BF16)
        hi_ref[1] = jnp.where(low, zero, x).astype(BF16)

    n_tiles = heads_per_step * SWA_HD // LANES
    tiles_per_kv = n_tiles // 2

    m_rows = tiles_per_kv * w
    row = lax.broadcasted_iota(jnp.int32, (m_rows, 1), 0)
    out_lane = lax.broadcasted_iota(jnp.int32, (m_rows, LANES), 1)

    def mask_bias(klen, is_first):
        qq = lax.broadcasted_iota(jnp.int32, (m_rows, klen), 0) % w
        kk = lax.broadcasted_iota(jnp.int32, (m_rows, klen), 1)
        valid = (kk <= qq) if is_first else ((kk > qq) & (kk <= qq + w))
        return jnp.where(valid, 0.0, NEG_INF).astype(F32)

    def sink_column(g, half):
        sink = jnp.zeros((m_rows, 1), F32)
        for ti in range(tiles_per_kv):
            head = pair * heads_per_step + 2 * (g * tiles_per_kv + ti) + half
            sink = jnp.where((row >= ti * w) & (row < (ti + 1) * w), sink_ref[head], sink)
        return sink

    sinks = [[sink_column(g, half) for half in range(2)] for g in range(2)]

    def block(r0, k0, klen, bias):
        for g in range(2):
            tiles = range(g * tiles_per_kv, (g + 1) * tiles_per_kv)
            q = jnp.concatenate([q_ref[pl.ds(r0, w), c * LANES:(c + 1) * LANES] for c in tiles], axis=0)
            kcat = jnp.concatenate([klo[g, pl.ds(k0, klen), :], khi[g, pl.ds(k0, klen), :]], axis=0)
            vcat = jnp.concatenate([vlo[g, pl.ds(k0, klen), :], vhi[g, pl.ds(k0, klen), :]], axis=0)
            s = _dot_nt(q, kcat)
            probs, inv_den = [], []
            for half in range(2):
                sink = sinks[g][half]
                sh = s[:, half * klen:(half + 1) * klen] + bias
                m = jnp.maximum(jnp.max(sh, axis=-1, keepdims=True), sink)
                p = jnp.exp(sh - m)
                inv_den.append(1.0 / (jnp.sum(p, axis=-1, keepdims=True) + jnp.exp(sink - m)))
                probs.append(p.astype(BF16))
            o = _dot(jnp.concatenate(probs, axis=1), vcat) * jnp.where(out_lane < SWA_HD, inv_den[0], inv_den[1])
            for ti, c in enumerate(tiles):
                o_ref[pl.ds(r0, w), c * LANES:(c + 1) * LANES] = o[ti * w:(ti + 1) * w].astype(o_ref.dtype)

    block(0, 0, w, mask_bias(w, True))
    band = mask_bias(2 * w, False)

    def body(n, carry):
        r0 = pl.multiple_of(n * w, w)
        block(r0, pl.multiple_of(r0 - w, w), 2 * w, band)
        return carry

    n_blk = seq // w
    lax.fori_loop(1, n_blk, body, 0, unroll=3 if (n_blk - 1) % 3 == 0 else 1)


def _swa_attention(z, sinks, d):
    t = d.tokens
    hps = 2 * (d.swa_heads // d.swa_kv_heads)
    qw = hps * SWA_HD
    n_pairs = _exact_div(d.swa_kv_heads, 2)
    jq, jk, jv = _exact_div(d.off_qs, qw), _exact_div(d.off_ks, LANES), _exact_div(d.off_vs, LANES)
    grid_spec = pltpu.PrefetchScalarGridSpec(
        num_scalar_prefetch=1,
        grid=(d.batch, n_pairs),
        in_specs=[
            pl.BlockSpec((d.seq, qw), lambda b, p, s: (b, jq + p)),
            pl.BlockSpec((d.seq, LANES), lambda b, p, s: (b, jk + p)),
            pl.BlockSpec((d.seq, LANES), lambda b, p, s: (b, jv + p)),
        ],
        out_specs=pl.BlockSpec((d.seq, qw), lambda b, p, s: (b, p)),
        scratch_shapes=[pltpu.VMEM((2, d.seq, LANES), BF16)] * 4,
    )
    return pl.pallas_call(
        functools.partial(_swa_kernel, seq=d.seq, window=d.window, heads_per_step=hps),
        out_shape=jax.ShapeDtypeStruct((t, d.swa_q), BF16),
        grid_spec=grid_spec,
        compiler_params=_cparams(("parallel", "parallel"), 48),
        name="swa_attention",
    )(sinks.astype(F32), z, z, z)


def _cross_router_kernel(h_ref, gc_ref, wq_ref, kvm_ref, wo_ref, gf_ref, wr_ref, br_ref,
                         h2_ref, xnf_ref, rt_ref, cnt_ref, run_ref, *, x_heads, n_groups, scale):
    tm = h_ref.shape[0]
    hx = x_heads * X_HD
    n_exp = n_groups * EPG

    @pl.when(pl.program_id(0) == 0)
    def _():
        run_ref[...] = jnp.zeros_like(run_ref)

    h = h_ref[...]
    hn = _rms(h, gc_ref[...]).astype(BF16)
    q = (_dot(hn, wq_ref[...]) * scale).astype(BF16)
    outs = []
    for hd in range(x_heads):
        kh = kvm_ref[:, hd * X_HD:(hd + 1) * X_HD]
        vh = kvm_ref[:, hx + hd * X_HD:hx + (hd + 1) * X_HD]
        s = _dot_nt(q[:, hd * X_HD:(hd + 1) * X_HD], kh)
        m = jnp.max(s, axis=-1, keepdims=True)
        p = jnp.exp(s - m)
        l = jnp.sum(p, axis=-1, keepdims=True)
        outs.append((_dot(p.astype(BF16), vh) / l).astype(BF16))
    h2 = h + _dot(jnp.concatenate(outs, axis=1), wo_ref[...])
    h2_ref[...] = h2
    xnf = _rms(h2, gf_ref[...])
    xnf_ref[...] = xnf

    logits = _dot(xnf.astype(BF16), wr_ref[...]) + br_ref[...]
    lane = lax.broadcasted_iota(jnp.int32, (tm, LANES), 1).astype(F32)
    big = float(LANES)
    is_group = (lane >= n_exp) & (lane < n_exp + n_groups)
    gl = jnp.where(is_group, logits, -jnp.inf)
    gmax = jnp.max(gl, axis=-1, keepdims=True)
    g_lane = jnp.min(jnp.where(gl == gmax, lane, big), axis=-1, keepdims=True)
    p_group = 1.0 / jnp.sum(jnp.where(is_group, jnp.exp(gl - gmax), 0.0), axis=-1, keepdims=True)
    e_lo = (g_lane - n_exp) * EPG
    in_group = (lane >= e_lo) & (lane < e_lo + EPG)
    el = jnp.where(in_group, logits, -jnp.inf)
    m1 = jnp.max(el, axis=-1, keepdims=True)
    i1 = jnp.min(jnp.where(el == m1, lane, big), axis=-1, keepdims=True)
    el2 = jnp.where(lane == i1, -jnp.inf, el)
    m2 = jnp.max(el2, axis=-1, keepdims=True)
    i2 = jnp.min(jnp.where(el2 == m2, lane, big), axis=-1, keepdims=True)
    w2 = jnp.exp(m2 - m1)
    gate1 = p_group / (1.0 + w2)
    gate2 = gate1 * w2

    hot1 = lane == i1
    hot2 = lane == i2
    onehot = jnp.where(hot1 | hot2, 1.0, 0.0)
    rr = lax.broadcasted_iota(jnp.int32, (tm, tm), 0)
    cc = lax.broadcasted_iota(jnp.int32, (tm, tm), 1)
    tri = jnp.where(cc < rr, 1.0, 0.0).astype(BF16)
    before = _dot(tri, onehot.astype(BF16)) + run_ref[...]
    rank1 = jnp.sum(jnp.where(hot1, before, 0.0), axis=-1, keepdims=True)
    rank2 = jnp.sum(jnp.where(hot2, before, 0.0), axis=-1, keepdims=True)
    run = run_ref[...] + jnp.sum(onehot, axis=0, keepdims=True)
    run_ref[...] = run
    cnt_ref[...] = run

    rt = jnp.where(lane == 0, i1, 0.0)
    rt = jnp.where(lane == 1, i2, rt)
    rt = jnp.where(lane == 2, gate1, rt)
    rt = jnp.where(lane == 3, gate2, rt)
    rt = jnp.where(lane == 4, rank1, rt)
    rt = jnp.where(lane == 5, rank2, rt)
    rt_ref[...] = rt


def _cross_router(h, g_cross, w_xq, kvm, w_xo, g_ffn, w_r, b_r, d):
    t = d.tokens
    tm = d.t_cross
    dm = d.d_model
    hx = d.x_heads * X_HD
    steps_per_batch = _exact_div(d.seq, tm)
    row = lambda i: (i, 0)
    fixed = lambda i: (0, 0)
    return pl.pallas_call(
        functools.partial(_cross_router_kernel, x_heads=d.x_heads, n_groups=d.n_groups, scale=X_HD ** -0.5),
        out_shape=(
            jax.ShapeDtypeStruct((t, dm), F32),
            jax.ShapeDtypeStruct((t, dm), F32),
            jax.ShapeDtypeStruct((t, LANES), F32),
            jax.ShapeDtypeStruct((1, LANES), F32),
        ),
        grid=(_exact_div(t, tm),),
        in_specs=[
            pl.BlockSpec((tm, dm), row),
            pl.BlockSpec((1, dm), fixed),
            pl.BlockSpec((dm, hx), fixed, pipeline_mode=pl.Buffered(1)),
            pl.BlockSpec((d.mem_len, 2 * hx), lambda i: (i // steps_per_batch, 0)),
            pl.BlockSpec((hx, dm), fixed, pipeline_mode=pl.Buffered(1)),
            pl.BlockSpec((1, dm), fixed),
            pl.BlockSpec((dm, LANES), fixed, pipeline_mode=pl.Buffered(1)),
            pl.BlockSpec((1, LANES), fixed),
        ],
        out_specs=(
            pl.BlockSpec((tm, dm), row),
            pl.BlockSpec((tm, dm), row),
            pl.BlockSpec((tm, LANES), row),
            pl.BlockSpec((1, LANES), fixed),
        ),
        scratch_shapes=[pltpu.VMEM((1, LANES), F32)],
        compiler_params=_cparams(("arbitrary",), 56),
        name="cross_attention_router",
    )(h, g_cross.reshape(1, dm), w_xq, kvm, w_xo, g_ffn.reshape(1, dm), w_r, b_r)


def _row_copy(src_hbm, src_row, dst_ref, dst_row, sem):
    return pltpu.make_async_copy(src_hbm.at[pl.ds(src_row, 1)], dst_ref.at[pl.ds(dst_row, 1)], sem)


def _dispatch_kernel(dest_ref, fill_ref, x_ref, xs_hbm, zero_ref, sem, zrow_sem, zblk_sem, *,
                     t_tok, n_experts, moe_block, n_blocks):
    base = pl.program_id(0) * t_tok

    @pl.when(pl.program_id(0) == 0)
    def _():
        zero_ref[...] = jnp.zeros_like(zero_ref)
        zero_block = lambda b: pltpu.make_async_copy(
            zero_ref, xs_hbm.at[pl.ds(pl.multiple_of(b * moe_block, moe_block), moe_block)], zblk_sem)

        def for_fill_rows(fn):
            for e in range(n_experts):
                lax.fori_loop(fill_ref[2 * e], fill_ref[2 * e + 1], fn, 0)

        def start_row(r, carry):
            _row_copy(zero_ref, 0, xs_hbm, r, zrow_sem).start()
            return carry

        def wait_row(r, carry):
            _row_copy(zero_ref, 0, xs_hbm, 0, zrow_sem).wait()
            return carry

        def start_block(b, carry):
            zero_block(b).start()
            return carry

        def wait_block(b, carry):
            zero_block(0).wait()
            return carry

        for_fill_rows(start_row)
        lax.fori_loop(fill_ref[2 * n_experts], n_blocks, start_block, 0)
        for_fill_rows(wait_row)
        lax.fori_loop(fill_ref[2 * n_experts], n_blocks, wait_block, 0)

    def issue(r, carry):
        for k in range(TOP_K):
            _row_copy(x_ref, r, xs_hbm, dest_ref[TOP_K * (base + r) + k], sem).start()
        return carry

    def drain(r, carry):
        for k in range(TOP_K):
            _row_copy(x_ref, 0, xs_hbm, 0, sem).wait()
        return carry

    lax.fori_loop(0, t_tok, issue, 0)
    lax.fori_loop(0, t_tok, drain, 0)


def _dispatch(dest, fill, x, d):
    t = d.tokens
    p_rows = d.n_blocks * d.moe_block
    grid_spec = pltpu.PrefetchScalarGridSpec(
        num_scalar_prefetch=2,
        grid=(_exact_div(t, d.t_tok),),
        in_specs=[pl.BlockSpec((d.t_tok, d.d_model), lambda i, dest, fill: (i, 0))],
        out_specs=pl.BlockSpec(memory_space=pl.ANY),
        scratch_shapes=[pltpu.VMEM((d.moe_block, d.d_model), x.dtype)] + [pltpu.SemaphoreType.DMA(())] * 3,
    )
    return pl.pallas_call(
        functools.partial(_dispatch_kernel, t_tok=d.t_tok, n_experts=d.n_experts, moe_block=d.moe_block,
                          n_blocks=d.n_blocks),
        out_shape=jax.ShapeDtypeStruct((p_rows, d.d_model), x.dtype),
        grid_spec=grid_spec,
        compiler_params=_cparams(("arbitrary",), 32),
        name="moe_dispatch",
    )(dest, fill, x)


W_SLABS = 4


def _moe_steps(d):
    return W_SLABS + d.n_blocks + W_SLABS * d.n_experts + 1


def _moe_schedule(padded, d):
    bm, ne, nblk = d.moe_block, d.n_experts, d.n_blocks
    i32 = jnp.int32
    nb = padded // bm
    first_blk = (jnp.cumsum(padded) - padded) // bm
    n_used = jnp.sum(nb)
    per_expert = jnp.where(jnp.arange(ne) == ne - 1, nb, jnp.maximum(nb, W_SLABS))
    step_end = W_SLABS + jnp.cumsum(per_expert)
    step_start = step_end - per_expert
    total = step_end[-1]
    s = jnp.arange(_moe_steps(d), dtype=i32)
    is_pro = s < W_SLABS
    is_tail = s >= total
    e = jnp.minimum(jnp.sum((step_end[None, :] <= s[:, None]).astype(i32), axis=1), ne - 1)
    j = jnp.where(is_pro, s, s - step_start[e])
    has_blk = jnp.logical_not(is_pro | is_tail) & (j < nb[e])
    blk = jnp.where(is_pro, 0, first_blk[e] + jnp.minimum(j, nb[e]))
    xblk = jnp.where(has_blk, blk, 0)
    oblk = jnp.where(is_tail, jnp.minimum(n_used + s - total, nblk), blk)
    nxt = jnp.where(is_pro, 0, jnp.minimum(e + 1, ne - 1))
    slab = jnp.where(is_tail, W_SLABS - 1, jnp.minimum(j, W_SLABS - 1))
    slot = jnp.where(is_pro, 1, e % 2)
    n_steps = total + (nblk + 1 - n_used)
    return tuple(v.astype(i32) for v in (xblk, oblk, nxt, slab, slot, is_tail)), n_steps.astype(i32)


def _silu_mul(a, b):
    return (a / (1.0 + jnp.exp(-a))) * b


def _expert_up_kernel(xblk, oblk, nxt, slab, slot, tail, x_ref, wg_st, wu_st, hb_ref, wg0, wg1, wu0, wu1):
    del xblk, oblk, nxt
    s = pl.program_id(0)
    rows = wg_st.shape[1]

    @pl.when(s == 0)
    def _():
        wg1[...] = jnp.zeros_like(wg1)
        wu1[...] = jnp.zeros_like(wu1)

    @pl.when(tail[s] == 1)
    def _():
        hb_ref[...] = jnp.zeros_like(hb_ref)

    r0 = pl.multiple_of(slab[s] * rows, rows)
    for cur, (wg_c, wu_c, wg_n, wu_n) in enumerate(((wg0, wu0, wg1, wu1), (wg1, wu1, wg0, wu0))):
        @pl.when((tail[s] == 0) & (slot[s] == cur))
        def _():
            wg_n[pl.ds(r0, rows), :] = wg_st[0].astype(BF16)
            wu_n[pl.ds(r0, rows), :] = wu_st[0].astype(BF16)
            x = x_ref[...].astype(BF16)
            hb_ref[...] = _silu_mul(_dot(x, wg_c[...]), _dot(x, wu_c[...])).astype(hb_ref.dtype)


def _expert_down_kernel(xblk, oblk, nxt, slab, slot, tail, hb_ref, wd_st, y_ref, wd0, wd1):
    del xblk, oblk, nxt
    s = pl.program_id(0)
    rows = wd_st.shape[1]

    @pl.when(s == 0)
    def _():
        wd1[...] = jnp.zeros_like(wd1)

    @pl.when(tail[s] == 1)
    def _():
        y_ref[...] = jnp.zeros_like(y_ref)

    r0 = pl.multiple_of(slab[s] * rows, rows)
    for cur, (wd_c, wd_n) in enumerate(((wd0, wd1), (wd1, wd0))):
        @pl.when((tail[s] == 0) & (slot[s] == cur))
        def _():
            wd_n[pl.ds(r0, rows), :] = wd_st[0].astype(BF16)
            y_ref[...] = _dot(hb_ref[...], wd_c[...])


def _experts(schedule, xs, w_gate, w_up, w_down, layer, d):
    sched, n_steps = schedule
    bm = d.moe_block
    dm, de = d.d_model, d.d_expert
    ne = d.n_experts
    rows_out = (d.n_blocks + 1) * bm
    w_gate, w_up = w_gate.reshape(-1, dm, de), w_up.reshape(-1, dm, de)
    w_down = w_down.reshape(-1, de, dm)
    x_map = lambda s, xb, ob, nx, sl, st, tl: (xb[s], 0)
    o_map = lambda s, xb, ob, nx, sl, st, tl: (ob[s], 0)
    w_map = lambda s, xb, ob, nx, sl, st, tl: (layer * ne + nx[s], sl[s], 0)
    ku, kd = _exact_div(dm, W_SLABS), _exact_div(de, W_SLABS)
    hb = pl.pallas_call(
        _expert_up_kernel,
        out_shape=jax.ShapeDtypeStruct((rows_out, de), BF16),
        grid_spec=pltpu.PrefetchScalarGridSpec(
            num_scalar_prefetch=len(sched),
            grid=(n_steps,),
            in_specs=[pl.BlockSpec((bm, dm), x_map), pl.BlockSpec((1, ku, de), w_map),
                      pl.BlockSpec((1, ku, de), w_map)],
            out_specs=pl.BlockSpec((bm, de), o_map),
            scratch_shapes=[pltpu.VMEM((dm, de), BF16)] * 4,
        ),
        compiler_params=_cparams(("arbitrary",), 56),
        name="moe_experts_up",
    )(*sched, xs, w_gate, w_up)
    return pl.pallas_call(
        _expert_down_kernel,
        out_shape=jax.ShapeDtypeStruct((rows_out, dm), F32),
        grid_spec=pltpu.PrefetchScalarGridSpec(
            num_scalar_prefetch=len(sched),
            grid=(n_steps,),
            in_specs=[pl.BlockSpec((bm, de), x_map), pl.BlockSpec((1, kd, dm), w_map)],
            out_specs=pl.BlockSpec((bm, dm), o_map),
            scratch_shapes=[pltpu.VMEM((de, dm), BF16)] * 2,
        ),
        compiler_params=_cparams(("arbitrary",), 40),
        name="moe_experts_down",
    )(*sched, hb, w_down)


def _combine_kernel(dest_ref, y_hbm, h_ref, rt_ref, g_ref, *rest, t_tok, last):
    if last:
        out_ref, ybuf, sem = rest
    else:
        h3_ref, out_ref, ybuf, sem = rest
    i = pl.program_id(0)
    buf = i % 2

    def gather(step, b):
        base = step * t_tok

        def issue(r, carry):
            for k in range(TOP_K):
                _row_copy(y_hbm, dest_ref[TOP_K * (base + r) + k], ybuf.at[b, k], r, sem.at[b]).start()
            return carry

        lax.fori_loop(0, t_tok, issue, 0)

    @pl.when(i == 0)
    def _():
        gather(0, 0)

    @pl.when(i + 1 < pl.num_programs(0))
    def _():
        gather(i + 1, 1 - buf)

    def drain(r, carry):
        for k in range(TOP_K):
            _row_copy(y_hbm, 0, ybuf.at[buf, k], 0, sem.at[buf]).wait()
        return carry

    lax.fori_loop(0, t_tok, drain, 0)
    rt = rt_ref[...]
    h3 = h_ref[...] + rt[:, 2:3] * ybuf[buf, 0] + rt[:, 3:4] * ybuf[buf, 1]
    if not last:
        h3_ref[...] = h3
    out_ref[...] = _rms(h3, g_ref[...]).astype(out_ref.dtype)


def _combine(dest, y, h2, rt, g_next, d, last):
    t = d.tokens
    tt = d.t_tok
    dm = d.d_model
    row = lambda i, ds: (i, 0)
    out_specs = pl.BlockSpec((tt, dm), row)
    if last:
        out_shape = jax.ShapeDtypeStruct((t, dm), F32)
    else:
        out_shape = (jax.ShapeDtypeStruct((t, dm), F32), jax.ShapeDtypeStruct((t, dm), BF16))
        out_specs = (out_specs, pl.BlockSpec((tt, dm), row))
    grid_spec = pltpu.PrefetchScalarGridSpec(
        num_scalar_prefetch=1,
        grid=(_exact_div(t, tt),),
        in_specs=[
            pl.BlockSpec(memory_space=pl.ANY),
            pl.BlockSpec((tt, dm), row),
            pl.BlockSpec((tt, LANES), row),
            pl.BlockSpec((1, dm), lambda i, ds: (0, 0)),
        ],
        out_specs=out_specs,
        scratch_shapes=[pltpu.VMEM((2, TOP_K, tt, dm), F32), pltpu.SemaphoreType.DMA((2,))],
    )
    return pl.pallas_call(
        functools.partial(_combine_kernel, t_tok=tt, last=last),
        out_shape=out_shape,
        grid_spec=grid_spec,
        compiler_params=_cparams(("arbitrary",), 48),
        name="moe_combine",
    )(dest, y, h2, rt, g_next.reshape(1, dm))


def _layer(h, xn, cos, sin, memn, p, d, g_next, last):
    t = d.tokens
    dm = d.d_model
    tm = d.tm
    row128 = pl.BlockSpec((tm, LANES), lambda i, j: (i, 0))

    z = _in_proj(xn, p["w_in"], p["layer"], cos, sin, d)
    kpe = _matmul(xn, p["w_kr"], k=dm, a_col=0, tm=tm, tn=LANES, out_dtype=BF16, post=_post_rope_all,
                  extras=((cos, row128), (sin, row128)), name="rope_key_proj")

    tn_q = min(4, d.mla_heads) * MLA_HEAD_PAD
    q_full = _matmul(
        z, p["w_uq"], k=d.q_lora, a_col=0, tm=tm, tn=tn_q, out_dtype=BF16, pre=_pre_rms,
        post=functools.partial(_post_mla_q, scale=(NOPE_DIM + ROPE_DIM) ** -0.5),
        extras=((p["g_qa"], pl.BlockSpec((1, d.q_lora), lambda i, j: (0, 0))), (cos, row128), (sin, row128)),
        name="mla_q_proj")
    kv = _matmul(
        z, p["w_ukv"], k=d.kv_lora, a_col=_exact_div(d.off_ckv, d.kv_lora), tm=tm, tn=tn_q, out_dtype=BF16,
        pre=_pre_rms, post=_post_cast,
        extras=((p["g_kva"], pl.BlockSpec((1, d.kv_lora), lambda i, j: (0, 0))),), name="mla_kv_proj")
    o_a = _mla_attention(q_full, kv, kpe, d)

    o_b = _swa_attention(z, p["sinks"], d)

    merged = _merge(o_a, p["w_pa"], o_b, p["w_pb"], z, d)
    h1 = _matmul(merged, p["w_o"], k=dm, a_col=0, tm=tm, tn=d.tn, out_dtype=F32, post=_post_residual,
                 extras=((h, pl.BlockSpec((tm, d.tn), lambda i, j: (i, j))),), name="out_proj")

    kvm = _matmul(memn, p["w_xkv"], k=dm, a_col=0, tm=memn.shape[0], tn=d.tn, out_dtype=BF16,
                  post=_post_cast, name="mem_kv_proj")
    h2, xnf, rt, cnt = _cross_router(h1, p["g_cross"], p["w_xq"], kvm, p["w_xo"], p["g_ffn"],
                                     p["w_r"], p["b_r"], d)

    bm = d.moe_block
    counts = cnt[0, :d.n_experts].astype(jnp.int32)
    padded = (counts + bm - 1) // bm * bm
    pad_ends = jnp.cumsum(padded)
    pad_starts = pad_ends - padded
    expert = rt[:, 0:TOP_K].astype(jnp.int32)
    dest = (pad_starts[expert] + rt[:, 4:4 + TOP_K].astype(jnp.int32)).reshape(t * TOP_K)
    fill = jnp.concatenate([jnp.stack([pad_starts + counts, pad_ends], axis=1).reshape(-1),
                            pad_ends[-1:] // bm]).astype(jnp.int32)

    xs = _dispatch(dest, fill, xnf, d)
    y = _experts(_moe_schedule(padded, d), xs, p["w_gate"], p["w_up"], p["w_down"], p["layer"], d)
    return _combine(dest, y, h2, rt, g_next, d, last)


def _prep_layer(l, d, g_mix, w_in, g_qa, g_kva, w_uq, w_ukv, sinks, w_pa, w_pb, w_o, g_cross, w_xq, w_xkv,
                w_xo, g_ffn, w_group, b_group, w_router, b_router, w_gate, w_up, w_down):
    dm = d.d_model
    kr0 = d.q_lora + d.kv_lora
    w = w_in[l]
    w_kr = jnp.pad(w[:, kr0:kr0 + ROPE_DIM], ((0, 0), (0, LANES - ROPE_DIM))).astype(BF16)
    qk = NOPE_DIM + ROPE_DIM
    wq = w_uq[l].reshape(d.q_lora, d.mla_heads, qk)
    wq = jnp.pad(wq, ((0, 0), (0, 0), (0, MLA_HEAD_PAD - qk))).reshape(d.q_lora, d.mla_heads * MLA_HEAD_PAD)
    n_r = d.n_experts + d.n_groups
    w_r = jnp.pad(jnp.concatenate([w_router[l], w_group[l]], axis=1), ((0, 0), (0, LANES - n_r)))
    b_r = jnp.pad(jnp.concatenate([b_router[l], b_group[l]]), (0, LANES - n_r)).reshape(1, LANES)
    return dict(
        w_in=w_in, w_kr=w_kr, w_uq=wq.astype(BF16), w_ukv=w_ukv[l].astype(BF16),
        g_qa=g_qa[l].reshape(1, -1).astype(F32), g_kva=g_kva[l].reshape(1, -1).astype(F32),
        sinks=sinks[l], w_pa=w_pa[l].astype(BF16), w_pb=w_pb[l].astype(BF16), w_o=w_o[l].astype(BF16),
        g_cross=g_cross[l].astype(F32), w_xq=w_xq[l].astype(BF16), w_xkv=w_xkv[l].astype(BF16),
        w_xo=w_xo[l].astype(BF16), g_ffn=g_ffn[l].astype(F32), w_r=w_r.astype(BF16), b_r=b_r.astype(F32),
        w_gate=w_gate, w_up=w_up, w_down=w_down, layer=l,
    )


def _forward(d, x, mem, positions, g_mix, w_in, g_qa, g_kva, w_uq, w_ukv, sinks, w_pa, w_pb, w_o,
             g_cross, g_mem, w_xq, w_xkv, w_xo, g_ffn, w_group, b_group, w_router, b_router,
             w_gate, w_up, w_down, g_final):
    depth = w_in.shape[0]
    t = d.tokens
    dm = d.d_model
    cos, sin = _rope_tables(positions, d)
    memn = _rmsnorm(mem.reshape(d.batch * d.mem_len, dm), g_mem, BF16, d.t_norm)
    h = x.reshape(t, dm)
    xn = _rmsnorm(h, g_mix[0], BF16, d.t_norm)
    for l in range(depth):
        p = _prep_layer(l, d, g_mix, w_in, g_qa, g_kva, w_uq, w_ukv, sinks, w_pa, w_pb, w_o, g_cross,
                        w_xq, w_xkv, w_xo, g_ffn, w_group, b_group, w_router, b_router, w_gate, w_up, w_down)
        last = l == depth - 1
        g_next = g_final if last else g_mix[l + 1]
        res = _layer(h, xn, cos, sin, memn, p, d, g_next, last)
        if last:
            return res.reshape(d.batch, d.seq, dm)
        h, xn = res


def kernel(x, mem, positions, g_mix, w_in, g_qa, g_kva, w_uq, w_ukv, sinks, w_pa, w_pb, w_o, g_cross, g_mem,
           w_xq, w_xkv, w_xo, g_ffn, w_group, b_group, w_router, b_router, w_gate, w_up, w_down, g_final):
    return _forward(Dims(), x, mem, positions, g_mix, w_in, g_qa, g_kva, w_uq, w_ukv, sinks, w_pa, w_pb, w_o,
                    g_cross, g_mem, w_xq, w_xkv, w_xo, g_ffn, w_group, b_group, w_router, b_router,
                    w_gate, w_up, w_down, g_final)
```

```python
import functools
from typing import NamedTuple

import jax
import jax.numpy as jnp
from jax import lax
from jax.experimental import pallas as pl
from jax.experimental.pallas import tpu as pltpu

F32 = jnp.float32
BF16 = jnp.bfloat16
EPS = 1e-6
ROPE_THETA = 10000.0
NEG_INF = -1e30
LANES = 128
ROPE_DIM = 64
NOPE_DIM = 128
MLA_V_DIM = 128
MLA_HEAD_PAD = 256
SWA_HD = 64
X_HD = 128
EPG = 8
TOP_K = 2
MIB = 1024 * 1024


class Dims(NamedTuple):
    batch: int = 4
    seq: int = 2048
    d_model: int = 4096
    mem_len: int = 256
    mla_heads: int = 16
    q_lora: int = 1024
    kv_lora: int = 512
    swa_heads: int = 32
    swa_kv_heads: int = 8
    window: int = 128
    x_heads: int = 4
    n_groups: int = 4
    d_expert: int = 768
    moe_block: int = 128
    tm: int = 1024
    tm_in: int = 2048
    tn_in: int = 512
    tn: int = 512
    tq: int = 512
    t_cross: int = 256
    t_tok: int = 256
    t_norm: int = 256

    @property
    def tokens(self):
        return self.batch * self.seq

    @property
    def n_experts(self):
        return self.n_groups * EPG

    @property
    def swa_q(self):
        return self.swa_heads * SWA_HD

    @property
    def swa_kv(self):
        return self.swa_kv_heads * SWA_HD

    @property
    def off_ckv(self):
        return self.q_lora

    @property
    def off_qs(self):
        return self.q_lora + self.kv_lora

    @property
    def off_ks(self):
        return self.off_qs + self.swa_q

    @property
    def off_vs(self):
        return self.off_ks + self.swa_kv

    @property
    def off_ga(self):
        return self.off_vs + self.swa_kv

    @property
    def off_gb(self):
        return self.off_ga + self.d_model

    @property
    def n_main(self):
        return self.off_gb + self.d_model

    @property
    def n_blocks(self):
        return -(-(self.tokens * TOP_K) // self.moe_block) + self.n_experts


def _exact_div(a, b):
    assert a % b == 0, (a, b)
    return a // b


def _cparams(sem, vmem_mib):
    return pltpu.CompilerParams(dimension_semantics=sem, vmem_limit_bytes=vmem_mib * MIB)


def _rms(x, g):
    return x * lax.rsqrt(jnp.mean(x * x, axis=-1, keepdims=True) + EPS) * g


def _rope128(x, cos, sin):
    lane = lax.broadcasted_iota(jnp.int32, x.shape, 1)
    first_half = (lane % ROPE_DIM) < (ROPE_DIM // 2)
    rot = jnp.where(first_half, -pltpu.roll(x, LANES - ROPE_DIM // 2, 1), pltpu.roll(x, ROPE_DIM // 2, 1))
    return x * cos + rot * sin


def _dot(a, b):
    return jnp.dot(a, b, preferred_element_type=F32)


def _dot_nt(a, b):
    return lax.dot_general(a, b, (((1,), (1,)), ((), ())), preferred_element_type=F32)


def _rope_table_kernel(pos_ref, inv_ref, cos_ref, sin_ref):
    ang = pos_ref[...].astype(F32) * inv_ref[...]
    cos_ref[...] = jnp.cos(ang)
    sin_ref[...] = jnp.sin(ang)


def _rope_tables(positions, d):
    t = d.tokens
    half = ROPE_DIM // 2
    inv_freq = 1.0 / (ROPE_THETA ** (jnp.arange(0, ROPE_DIM, 2, dtype=F32) / ROPE_DIM))
    inv = jnp.tile(inv_freq, LANES // half).reshape(1, LANES)
    tb = min(t, 1024)
    return pl.pallas_call(
        _rope_table_kernel,
        out_shape=(jax.ShapeDtypeStruct((t, LANES), F32),) * 2,
        grid=(_exact_div(t, tb),),
        in_specs=[pl.BlockSpec((tb, 1), lambda i: (i, 0)), pl.BlockSpec((1, LANES), lambda i: (0, 0))],
        out_specs=(pl.BlockSpec((tb, LANES), lambda i: (i, 0)),) * 2,
        name="rope_tables",
    )(positions.reshape(t, 1), inv)


def _rmsnorm_kernel(x_ref, g_ref, o_ref):
    o_ref[...] = _rms(x_ref[...].astype(F32), g_ref[...]).astype(o_ref.dtype)


def _rmsnorm(x, g, out_dtype, tm):
    m, dd = x.shape
    return pl.pallas_call(
        _rmsnorm_kernel,
        out_shape=jax.ShapeDtypeStruct((m, dd), out_dtype),
        grid=(_exact_div(m, tm),),
        in_specs=[pl.BlockSpec((tm, dd), lambda i: (i, 0)), pl.BlockSpec((1, dd), lambda i: (0, 0))],
        out_specs=pl.BlockSpec((tm, dd), lambda i: (i, 0)),
        compiler_params=_cparams(("parallel",), 32),
        name="rmsnorm",
    )(x, g.reshape(1, dd).astype(F32))


def _mm_kernel(*refs, pre, post, n_extra):
    a_ref, w_ref = refs[0], refs[1]
    extras = refs[2:2 + n_extra]
    out_ref = refs[2 + n_extra]
    a = a_ref[...]
    if pre is not None:
        a = pre(a, extras)
    post(_dot(a, w_ref[...]), extras, out_ref)


def _matmul(a, w, *, k, a_col, tm, tn, out_dtype, post, pre=None, extras=(), vmem_mib=48, name):
    m = a.shape[0]
    n = w.shape[1]
    assert w.shape[0] == k
    in_specs = [pl.BlockSpec((tm, k), lambda i, j: (i, a_col)), pl.BlockSpec((k, tn), lambda i, j: (0, j))]
    in_specs += [s for _, s in extras]
    return pl.pallas_call(
        functools.partial(_mm_kernel, pre=pre, post=post, n_extra=len(extras)),
        out_shape=jax.ShapeDtypeStruct((m, n), out_dtype),
        grid=(_exact_div(m, tm), _exact_div(n, tn)),
        in_specs=in_specs,
        out_specs=pl.BlockSpec((tm, tn), lambda i, j: (i, j)),
        compiler_params=_cparams(("parallel", "arbitrary"), vmem_mib),
        name=name,
    )(a, w, *[x for x, _ in extras])


def _post_cast(acc, extras, o_ref):
    o_ref[...] = acc.astype(o_ref.dtype)


def _pre_rms(a, extras):
    return _rms(a.astype(F32), extras[0][...]).astype(BF16)


def _post_residual(acc, extras, o_ref):
    o_ref[...] = extras[0][...] + acc


def _post_in_proj(acc, extras, o_ref, *, j_rope0, j_k, j_rope1, tn, q_scale):
    cos_ref, sin_ref = extras
    j = pl.program_id(1)
    is_rope = (j >= j_rope0) & (j < j_rope1)

    @pl.when(is_rope)
    def _():
        scale = jnp.where(j < j_k, q_scale, 1.0).astype(F32)
        cos = cos_ref[...]
        sin = sin_ref[...]
        for c in range(tn // LANES):
            sl = slice(c * LANES, (c + 1) * LANES)
            o_ref[:, sl] = (_rope128(acc[:, sl], cos, sin) * scale).astype(o_ref.dtype)

    @pl.when(jnp.logical_not(is_rope))
    def _():
        o_ref[...] = acc.astype(o_ref.dtype)


def _post_rope_all(acc, extras, o_ref):
    cos_ref, sin_ref = extras
    cos = cos_ref[...]
    sin = sin_ref[...]
    for c in range(acc.shape[1] // LANES):
        sl = slice(c * LANES, (c + 1) * LANES)
        o_ref[:, sl] = _rope128(acc[:, sl], cos, sin).astype(o_ref.dtype)


def _post_mla_q(acc, extras, o_ref, *, scale):
    _, cos_ref, sin_ref = extras
    cos = cos_ref[...]
    sin = sin_ref[...]
    for hd in range(acc.shape[1] // MLA_HEAD_PAD):
        lo = slice(hd * MLA_HEAD_PAD, hd * MLA_HEAD_PAD + LANES)
        hi = slice(hd * MLA_HEAD_PAD + LANES, (hd + 1) * MLA_HEAD_PAD)
        o_ref[:, lo] = (acc[:, lo] * scale).astype(o_ref.dtype)
        o_ref[:, hi] = (_rope128(acc[:, hi], cos, sin) * scale).astype(o_ref.dtype)


def _merge_kernel(oa_ref, wpa_ref, ob_ref, wpb_ref, ga_ref, gb_ref, o_ref):
    pa = _dot(oa_ref[...], wpa_ref[...])
    pb = _dot(ob_ref[...], wpb_ref[...])
    sa = 1.0 / (1.0 + jnp.exp(-ga_ref[...].astype(F32)))
    sb = 1.0 / (1.0 + jnp.exp(-gb_ref[...].astype(F32)))
    o_ref[...] = (sa * pa + sb * pb).astype(o_ref.dtype)


def _merge(o_a, w_pa, o_b, w_pb, z, d):
    t = d.tokens
    tm, tn = d.tm, d.tn
    ka, kb = o_a.shape[1], o_b.shape[1]
    ja, jb = _exact_div(d.off_ga, tn), _exact_div(d.off_gb, tn)
    return pl.pallas_call(
        _merge_kernel,
        out_shape=jax.ShapeDtypeStruct((t, d.d_model), BF16),
        grid=(_exact_div(t, tm), _exact_div(d.d_model, tn)),
        in_specs=[
            pl.BlockSpec((tm, ka), lambda i, j: (i, 0)),
            pl.BlockSpec((ka, tn), lambda i, j: (0, j)),
            pl.BlockSpec((tm, kb), lambda i, j: (i, 0)),
            pl.BlockSpec((kb, tn), lambda i, j: (0, j)),
            pl.BlockSpec((tm, tn), lambda i, j: (i, j + ja)),
            pl.BlockSpec((tm, tn), lambda i, j: (i, j + jb)),
        ],
        out_specs=pl.BlockSpec((tm, tn), lambda i, j: (i, j)),
        compiler_params=_cparams(("parallel", "arbitrary"), 48),
        name="gated_merge",
    )(o_a, w_pa, o_b, w_pb, z, z)


def _mla_kernel(q_ref, kv_ref, kpe_ref, o_ref, k_scr, *, seq, tq):
    k_scr[:, :NOPE_DIM] = kv_ref[:, :NOPE_DIM]
    k_scr[:, NOPE_DIM:] = kpe_ref[...]
    for i in range(seq // tq):
        ln = (i + 1) * tq
        q = q_ref[i * tq:(i + 1) * tq, :]
        s = _dot_nt(q, k_scr[:ln, :])
        row = lax.broadcasted_iota(jnp.int32, (tq, ln), 0) + i * tq
        col = lax.broadcasted_iota(jnp.int32, (tq, ln), 1)
        s = jnp.where(col <= row, s, NEG_INF)
        m = jnp.max(s, axis=-1, keepdims=True)
        p = jnp.exp(s - m)
        l = jnp.sum(p, axis=-1, keepdims=True)
        o = _dot(p.astype(BF16), kv_ref[:ln, NOPE_DIM:])
        o_ref[i * tq:(i + 1) * tq, :] = (o / l).astype(o_ref.dtype)


def _mla_attention(q_full, kv, kpe, d):
    t = d.tokens
    return pl.pallas_call(
        functools.partial(_mla_kernel, seq=d.seq, tq=d.tq),
        out_shape=jax.ShapeDtypeStruct((t, d.mla_heads * MLA_V_DIM), BF16),
        grid=(d.batch, d.mla_heads),
        in_specs=[
            pl.BlockSpec((d.seq, MLA_HEAD_PAD), lambda b, h: (b, h)),
            pl.BlockSpec((d.seq, NOPE_DIM + MLA_V_DIM), lambda b, h: (b, h)),
            pl.BlockSpec((d.seq, LANES), lambda b, h: (b, 0)),
        ],
        out_specs=pl.BlockSpec((d.seq, MLA_V_DIM), lambda b, h: (b, h)),
        scratch_shapes=[pltpu.VMEM((d.seq, MLA_HEAD_PAD), BF16)],
        compiler_params=_cparams(("parallel", "parallel"), 48),
        name="mla_attention",
    )(q_full, kv, kpe)


def _swa_kernel(sink_ref, q_ref, k_ref, v_ref, o_ref, klo, khi, vlo, vhi, *, seq, window, heads_per_step):
    pair = pl.program_id(1)
    w = window
    lane = lax.broadcasted_iota(jnp.int32, (seq, LANES), 1)
    low = lane < SWA_HD
    for src_ref, lo_ref, hi_ref in ((k_ref, klo, khi), (v_ref, vlo, vhi)):
        x = src_ref[...].astype(F32)
        xs = pltpu.roll(x, SWA_HD, 1)
        zero = jnp.zeros_like(x)
        lo_ref[0] = jnp.where(low, x, zero).astype(BF16)
        hi_ref[0] = jnp.where(low, zero, xs).astype(BF16)
        lo_ref[1] = jnp.where(low, xs, zero).astype(BF16)
        hi_ref[1] = jnp.where(low, zero, x).astype(BF16)

    n_tiles = heads_per_step * SWA_HD // LANES
    tiles_per_kv = n_tiles // 2

    m_rows = tiles_per_kv * w
    row = lax.broadcasted_iota(jnp.int32, (m_rows, 1), 0)
    out_lane = lax.broadcasted_iota(jnp.int32, (m_rows, LANES), 1)

    def mask_bias(klen, is_first):
        qq = lax.broadcasted_iota(jnp.int32, (m_rows, klen), 0) % w
        kk = lax.broadcasted_iota(jnp.int32, (m_rows, klen), 1)
        valid = (kk <= qq) if is_first else ((kk > qq) & (kk <= qq + w))
        return jnp.where(valid, 0.0, NEG_INF).astype(F32)

    def sink_column(g, half):
        sink = jnp.zeros((m_rows, 1), F32)
        for ti in range(tiles_per_kv):
            head = pair * heads_per_step + 2 * (g * tiles_per_kv + ti) + half
            sink = jnp.where((row >= ti * w) & (row < (ti + 1) * w), sink_ref[head], sink)
        return sink

    sinks = [[sink_column(g, half) for half in range(2)] for g in range(2)]

    def block(r0, k0, klen, bias):
        for g in range(2):
            tiles = range(g * tiles_per_kv, (g + 1) * tiles_per_kv)
            q = jnp.concatenate([q_ref[pl.ds(r0, w), c * LANES:(c + 1) * LANES] for c in tiles], axis=0)
            kcat = jnp.concatenate([klo[g, pl.ds(k0, klen), :], khi[g, pl.ds(k0, klen), :]], axis=0)
            vcat = jnp.concatenate([vlo[g, pl.ds(k0, klen), :], vhi[g, pl.ds(k0, klen), :]], axis=0)
            s = _dot_nt(q, kcat)
            probs, inv_den = [], []
            for half in range(2):
                sink = sinks[g][half]
                sh = s[:, half * klen:(half + 1) * klen] + bias
                m = jnp.maximum(jnp.max(sh, axis=-1, keepdims=True), sink)
                p = jnp.exp(sh - m)
                inv_den.append(1.0 / (jnp.sum(p, axis=-1, keepdims=True) + jnp.exp(sink - m)))
                probs.append(p.astype(BF16))
            o = _dot(jnp.concatenate(probs, axis=1), vcat) * jnp.where(out_lane < SWA_HD, inv_den[0], inv_den[1])
            for ti, c in enumerate(tiles):
                o_ref[pl.ds(r0, w), c * LANES:(c + 1) * LANES] = o[ti * w:(ti + 1) * w].astype(o_ref.dtype)

    block(0, 0, w, mask_bias(w, True))
    band = mask_bias(2 * w, False)

    def body(n, carry):
        r0 = pl.multiple_of(n * w, w)
        block(r0, pl.multiple_of(r0 - w, w), 2 * w, band)
        return carry

    n_blk = seq // w
    lax.fori_loop(1, n_blk, body, 0, unroll=3 if (n_blk - 1) % 3 == 0 else 1)


def _swa_attention(z, sinks, d):
    t = d.tokens
    hps = 2 * (d.swa_heads // d.swa_kv_heads)
    qw = hps * SWA_HD
    n_pairs = _exact_div(d.swa_kv_heads, 2)
    jq, jk, jv = _exact_div(d.off_qs, qw), _exact_div(d.off_ks, LANES), _exact_div(d.off_vs, LANES)
    grid_spec = pltpu.PrefetchScalarGridSpec(
        num_scalar_prefetch=1,
        grid=(d.batch, n_pairs),
        in_specs=[
            pl.BlockSpec((d.seq, qw), lambda b, p, s: (b, jq + p)),
            pl.BlockSpec((d.seq, LANES), lambda b, p, s: (b, jk + p)),
            pl.BlockSpec((d.seq, LANES), lambda b, p, s: (b, jv + p)),
        ],
        out_specs=pl.BlockSpec((d.seq, qw), lambda b, p, s: (b, p)),
        scratch_shapes=[pltpu.VMEM((2, d.seq, LANES), BF16)] * 4,
    )
    return pl.pallas_call(
        functools.partial(_swa_kernel, seq=d.seq, window=d.window, heads_per_step=hps),
        out_shape=jax.ShapeDtypeStruct((t, d.swa_q), BF16),
        grid_spec=grid_spec,
        compiler_params=_cparams(("parallel", "parallel"), 48),
        name="swa_attention",
    )(sinks.astype(F32), z, z, z)


def _cross_router_kernel(h_ref, gc_ref, wq_ref, kvm_ref, wo_ref, gf_ref, wr_ref, br_ref,
                         h2_ref, xnf_ref, rt_ref, cnt_ref, run_ref, *, x_heads, n_groups, scale):
    tm = h_ref.shape[0]
    hx = x_heads * X_HD
    n_exp = n_groups * EPG

    @pl.when(pl.program_id(0) == 0)
    def _():
        run_ref[...] = jnp.zeros_like(run_ref)

    h = h_ref[...]
    hn = _rms(h, gc_ref[...]).astype(BF16)
    q = (_dot(hn, wq_ref[...]) * scale).astype(BF16)
    outs = []
    for hd in range(x_heads):
        kh = kvm_ref[:, hd * X_HD:(hd + 1) * X_HD]
        vh = kvm_ref[:, hx + hd * X_HD:hx + (hd + 1) * X_HD]
        s = _dot_nt(q[:, hd * X_HD:(hd + 1) * X_HD], kh)
        m = jnp.max(s, axis=-1, keepdims=True)
        p = jnp.exp(s - m)
        l = jnp.sum(p, axis=-1, keepdims=True)
        outs.append((_dot(p.astype(BF16), vh) / l).astype(BF16))
    h2 = h + _dot(jnp.concatenate(outs, axis=1), wo_ref[...])
    h2_ref[...] = h2
    xnf = _rms(h2, gf_ref[...])
    xnf_ref[...] = xnf

    logits = _dot(xnf.astype(BF16), wr_ref[...]) + br_ref[...]
    lane = lax.broadcasted_iota(jnp.int32, (tm, LANES), 1).astype(F32)
    big = float(LANES)
    is_group = (lane >= n_exp) & (lane < n_exp + n_groups)
    gl = jnp.where(is_group, logits, -jnp.inf)
    gmax = jnp.max(gl, axis=-1, keepdims=True)
    g_lane = jnp.min(jnp.where(gl == gmax, lane, big), axis=-1, keepdims=True)
    p_group = 1.0 / jnp.sum(jnp.where(is_group, jnp.exp(gl - gmax), 0.0), axis=-1, keepdims=True)
    e_lo = (g_lane - n_exp) * EPG
    in_group = (lane >= e_lo) & (lane < e_lo + EPG)
    el = jnp.where(in_group, logits, -jnp.inf)
    m1 = jnp.max(el, axis=-1, keepdims=True)
    i1 = jnp.min(jnp.where(el == m1, lane, big), axis=-1, keepdims=True)
    el2 = jnp.where(lane == i1, -jnp.inf, el)
    m2 = jnp.max(el2, axis=-1, keepdims=True)
    i2 = jnp.min(jnp.where(el2 == m2, lane, big), axis=-1, keepdims=True)
    w2 = jnp.exp(m2 - m1)
    gate1 = p_group / (1.0 + w2)
    gate2 = gate1 * w2

    hot1 = lane == i1
    hot2 = lane == i2
    onehot = jnp.where(hot1 | hot2, 1.0, 0.0)
    rr = lax.broadcasted_iota(jnp.int32, (tm, tm), 0)
    cc = lax.broadcasted_iota(jnp.int32, (tm, tm), 1)
    tri = jnp.where(cc < rr, 1.0, 0.0).astype(BF16)
    before = _dot(tri, onehot.astype(BF16)) + run_ref[...]
    rank1 = jnp.sum(jnp.where(hot1, before, 0.0), axis=-1, keepdims=True)
    rank2 = jnp.sum(jnp.where(hot2, before, 0.0), axis=-1, keepdims=True)
    run = run_ref[...] + jnp.sum(onehot, axis=0, keepdims=True)
    run_ref[...] = run
    cnt_ref[...] = run

    rt = jnp.where(lane == 0, i1, 0.0)
    rt = jnp.where(lane == 1, i2, rt)
    rt = jnp.where(lane == 2, gate1, rt)
    rt = jnp.where(lane == 3, gate2, rt)
    rt = jnp.where(lane == 4, rank1, rt)
    rt = jnp.where(lane == 5, rank2, rt)
    rt_ref[...] = rt


def _cross_router(h, g_cross, w_xq, kvm, w_xo, g_ffn, w_r, b_r, d):
    t = d.tokens
    tm = d.t_cross
    dm = d.d_model
    hx = d.x_heads * X_HD
    steps_per_batch = _exact_div(d.seq, tm)
    row = lambda i: (i, 0)
    fixed = lambda i: (0, 0)
    return pl.pallas_call(
        functools.partial(_cross_router_kernel, x_heads=d.x_heads, n_groups=d.n_groups, scale=X_HD ** -0.5),
        out_shape=(
            jax.ShapeDtypeStruct((t, dm), F32),
            jax.ShapeDtypeStruct((t, dm), F32),
            jax.ShapeDtypeStruct((t, LANES), F32),
            jax.ShapeDtypeStruct((1, LANES), F32),
        ),
        grid=(_exact_div(t, tm),),
        in_specs=[
            pl.BlockSpec((tm, dm), row),
            pl.BlockSpec((1, dm), fixed),
            pl.BlockSpec((dm, hx), fixed, pipeline_mode=pl.Buffered(1)),
            pl.BlockSpec((d.mem_len, 2 * hx), lambda i: (i // steps_per_batch, 0)),
            pl.BlockSpec((hx, dm), fixed, pipeline_mode=pl.Buffered(1)),
            pl.BlockSpec((1, dm), fixed),
            pl.BlockSpec((dm, LANES), fixed, pipeline_mode=pl.Buffered(1)),
            pl.BlockSpec((1, LANES), fixed),
        ],
        out_specs=(
            pl.BlockSpec((tm, dm), row),
            pl.BlockSpec((tm, dm), row),
            pl.BlockSpec((tm, LANES), row),
            pl.BlockSpec((1, LANES), fixed),
        ),
        scratch_shapes=[pltpu.VMEM((1, LANES), F32)],
        compiler_params=_cparams(("arbitrary",), 56),
        name="cross_attention_router",
    )(h, g_cross.reshape(1, dm), w_xq, kvm, w_xo, g_ffn.reshape(1, dm), w_r, b_r)


def _row_copy(src_hbm, src_row, dst_ref, dst_row, sem):
    return pltpu.make_async_copy(src_hbm.at[pl.ds(src_row, 1)], dst_ref.at[pl.ds(dst_row, 1)], sem)


def _dispatch_kernel(dest_ref, fill_ref, x_ref, xs_hbm, zero_ref, sem, zrow_sem, zblk_sem, *,
                     t_tok, n_experts, moe_block, n_blocks):
    base = pl.program_id(0) * t_tok

    @pl.when(pl.program_id(0) == 0)
    def _():
        zero_ref[...] = jnp.zeros_like(zero_ref)
        zero_block = lambda b: pltpu.make_async_copy(
            zero_ref, xs_hbm.at[pl.ds(pl.multiple_of(b * moe_block, moe_block), moe_block)], zblk_sem)

        def for_fill_rows(fn):
            for e in range(n_experts):
                lax.fori_loop(fill_ref[2 * e], fill_ref[2 * e + 1], fn, 0)

        def start_row(r, carry):
            _row_copy(zero_ref, 0, xs_hbm, r, zrow_sem).start()
            return carry

        def wait_row(r, carry):
            _row_copy(zero_ref, 0, xs_hbm, 0, zrow_sem).wait()
            return carry

        def start_block(b, carry):
            zero_block(b).start()
            return carry

        def wait_block(b, carry):
            zero_block(0).wait()
            return carry

        for_fill_rows(start_row)
        lax.fori_loop(fill_ref[2 * n_experts], n_blocks, start_block, 0)
        for_fill_rows(wait_row)
        lax.fori_loop(fill_ref[2 * n_experts], n_blocks, wait_block, 0)

    def issue(r, carry):
        for k in range(TOP_K):
            _row_copy(x_ref, r, xs_hbm, dest_ref[TOP_K * (base + r) + k], sem).start()
        return carry

    def drain(r, carry):
        for k in range(TOP_K):
            _row_copy(x_ref, 0, xs_hbm, 0, sem).wait()
        return carry

    lax.fori_loop(0, t_tok, issue, 0)
    lax.fori_loop(0, t_tok, drain, 0)


def _dispatch(dest, fill, x, d):
    t = d.tokens
    p_rows = d.n_blocks * d.moe_block
    grid_spec = pltpu.PrefetchScalarGridSpec(
        num_scalar_prefetch=2,
        grid=(_exact_div(t, d.t_tok),),
        in_specs=[pl.BlockSpec((d.t_tok, d.d_model), lambda i, dest, fill: (i, 0))],
        out_specs=pl.BlockSpec(memory_space=pl.ANY),
        scratch_shapes=[pltpu.VMEM((d.moe_block, d.d_model), x.dtype)] + [pltpu.SemaphoreType.DMA(())] * 3,
    )
    return pl.pallas_call(
        functools.partial(_dispatch_kernel, t_tok=d.t_tok, n_experts=d.n_experts, moe_block=d.moe_block,
                          n_blocks=d.n_blocks),
        out_shape=jax.ShapeDtypeStruct((p_rows, d.d_model), x.dtype),
        grid_spec=grid_spec,
        compiler_params=_cparams(("arbitrary",), 32),
        name="moe_dispatch",
    )(dest, fill, x)


W_SLABS = 4


def _moe_steps(d):
    return W_SLABS + d.n_blocks + W_SLABS * d.n_experts + 1


def _moe_schedule(padded, d):
    bm, ne, nblk = d.moe_block, d.n_experts, d.n_blocks
    i32 = jnp.int32
    nb = padded // bm
    first_blk = (jnp.cumsum(padded) - padded) // bm
    n_used = jnp.sum(nb)
    per_expert = jnp.where(jnp.arange(ne) == ne - 1, nb, jnp.maximum(nb, W_SLABS))
    step_end = W_SLABS + jnp.cumsum(per_expert)
    step_start = step_end - per_expert
    total = step_end[-1]
    s = jnp.arange(_moe_steps(d), dtype=i32)
    is_pro = s < W_SLABS
    is_tail = s >= total
    e = jnp.minimum(jnp.sum((step_end[None, :] <= s[:, None]).astype(i32), axis=1), ne - 1)
    j = jnp.where(is_pro, s, s - step_start[e])
    has_blk = jnp.logical_not(is_pro | is_tail) & (j < nb[e])
    blk = jnp.where(is_pro, 0, first_blk[e] + jnp.minimum(j, nb[e]))
    xblk = jnp.where(has_blk, blk, 0)
    oblk = jnp.where(is_tail, jnp.minimum(n_used + s - total, nblk), blk)
    nxt = jnp.where(is_pro, 0, jnp.minimum(e + 1, ne - 1))
    slab = jnp.where(is_tail, W_SLABS - 1, jnp.minimum(j, W_SLABS - 1))
    slot = jnp.where(is_pro, 1, e % 2)
    return tuple(v.astype(i32) for v in (xblk, oblk, nxt, slab, slot, is_tail))


def _silu_mul(a, b):
    return (a / (1.0 + jnp.exp(-a))) * b


def _expert_up_kernel(xblk, oblk, nxt, slab, slot, tail, x_ref, wg_st, wu_st, hb_ref, wg0, wg1, wu0, wu1):
    del xblk, oblk, nxt
    s = pl.program_id(0)
    rows = wg_st.shape[1]

    @pl.when(s == 0)
    def _():
        wg1[...] = jnp.zeros_like(wg1)
        wu1[...] = jnp.zeros_like(wu1)

    @pl.when(tail[s] == 1)
    def _():
        hb_ref[...] = jnp.zeros_like(hb_ref)

    r0 = pl.multiple_of(slab[s] * rows, rows)
    for cur, (wg_c, wu_c, wg_n, wu_n) in enumerate(((wg0, wu0, wg1, wu1), (wg1, wu1, wg0, wu0))):
        @pl.when((tail[s] == 0) & (slot[s] == cur))
        def _():
            wg_n[pl.ds(r0, rows), :] = wg_st[0].astype(BF16)
            wu_n[pl.ds(r0, rows), :] = wu_st[0].astype(BF16)
            x = x_ref[...].astype(BF16)
            hb_ref[...] = _silu_mul(_dot(x, wg_c[...]), _dot(x, wu_c[...])).astype(hb_ref.dtype)


def _expert_down_kernel(xblk, oblk, nxt, slab, slot, tail, hb_ref, wd_st, y_ref, wd0, wd1):
    del xblk, oblk, nxt
    s = pl.program_id(0)
    rows = wd_st.shape[1]

    @pl.when(s == 0)
    def _():
        wd1[...] = jnp.zeros_like(wd1)

    @pl.when(tail[s] == 1)
    def _():
        y_ref[...] = jnp.zeros_like(y_ref)

    r0 = pl.multiple_of(slab[s] * rows, rows)
    for cur, (wd_c, wd_n) in enumerate(((wd0, wd1), (wd1, wd0))):
        @pl.when((tail[s] == 0) & (slot[s] == cur))
        def _():
            wd_n[pl.ds(r0, rows), :] = wd_st[0].astype(BF16)
            y_ref[...] = _dot(hb_ref[...], wd_c[...])


def _experts(sched, xs, w_gate, w_up, w_down, layer, d):
    bm = d.moe_block
    dm, de = d.d_model, d.d_expert
    ne = d.n_experts
    rows_out = (d.n_blocks + 1) * bm
    w_gate, w_up = w_gate.reshape(-1, dm, de), w_up.reshape(-1, dm, de)
    w_down = w_down.reshape(-1, de, dm)
    x_map = lambda s, xb, ob, nx, sl, st, tl: (xb[s], 0)
    o_map = lambda s, xb, ob, nx, sl, st, tl: (ob[s], 0)
    w_map = lambda s, xb, ob, nx, sl, st, tl: (layer * ne + nx[s], sl[s], 0)
    ku, kd = _exact_div(dm, W_SLABS), _exact_div(de, W_SLABS)
    hb = pl.pallas_call(
        _expert_up_kernel,
        out_shape=jax.ShapeDtypeStruct((rows_out, de), BF16),
        grid_spec=pltpu.PrefetchScalarGridSpec(
            num_scalar_prefetch=len(sched),
            grid=(_moe_steps(d),),
            in_specs=[pl.BlockSpec((bm, dm), x_map), pl.BlockSpec((1, ku, de), w_map),
                      pl.BlockSpec((1, ku, de), w_map)],
            out_specs=pl.BlockSpec((bm, de), o_map),
            scratch_shapes=[pltpu.VMEM((dm, de), BF16)] * 4,
        ),
        compiler_params=_cparams(("arbitrary",), 56),
        name="moe_experts_up",
    )(*sched, xs, w_gate, w_up)
    return pl.pallas_call(
        _expert_down_kernel,
        out_shape=jax.ShapeDtypeStruct((rows_out, dm), F32),
        grid_spec=pltpu.PrefetchScalarGridSpec(
            num_scalar_prefetch=len(sched),
            grid=(_moe_steps(d),),
            in_specs=[pl.BlockSpec((bm, de), x_map), pl.BlockSpec((1, kd, dm), w_map)],
            out_specs=pl.BlockSpec((bm, dm), o_map),
            scratch_shapes=[pltpu.VMEM((de, dm), BF16)] * 2,
        ),
        compiler_params=_cparams(("arbitrary",), 40),
        name="moe_experts_down",
    )(*sched, hb, w_down)


def _combine_kernel(dest_ref, y_hbm, h_ref, rt_ref, g_ref, *rest, t_tok, last):
    if last:
        out_ref, ybuf, sem = rest
    else:
        h3_ref, out_ref, ybuf, sem = rest
    i = pl.program_id(0)
    buf = i % 2

    def gather(step, b):
        base = step * t_tok

        def issue(r, carry):
            for k in range(TOP_K):
                _row_copy(y_hbm, dest_ref[TOP_K * (base + r) + k], ybuf.at[b, k], r, sem.at[b]).start()
            return carry

        lax.fori_loop(0, t_tok, issue, 0)

    @pl.when(i == 0)
    def _():
        gather(0, 0)

    @pl.when(i + 1 < pl.num_programs(0))
    def _():
        gather(i + 1, 1 - buf)

    def drain(r, carry):
        for k in range(TOP_K):
            _row_copy(y_hbm, 0, ybuf.at[buf, k], 0, sem.at[buf]).wait()
        return carry

    lax.fori_loop(0, t_tok, drain, 0)
    rt = rt_ref[...]
    h3 = h_ref[...] + rt[:, 2:3] * ybuf[buf, 0] + rt[:, 3:4] * ybuf[buf, 1]
    if not last:
        h3_ref[...] = h3
    out_ref[...] = _rms(h3, g_ref[...]).astype(out_ref.dtype)


def _combine(dest, y, h2, rt, g_next, d, last):
    t = d.tokens
    tt = d.t_tok
    dm = d.d_model
    row = lambda i, ds: (i, 0)
    out_specs = pl.BlockSpec((tt, dm), row)
    if last:
        out_shape = jax.ShapeDtypeStruct((t, dm), F32)
    else:
        out_shape = (jax.ShapeDtypeStruct((t, dm), F32), jax.ShapeDtypeStruct((t, dm), BF16))
        out_specs = (out_specs, pl.BlockSpec((tt, dm), row))
    grid_spec = pltpu.PrefetchScalarGridSpec(
        num_scalar_prefetch=1,
        grid=(_exact_div(t, tt),),
        in_specs=[
            pl.BlockSpec(memory_space=pl.ANY),
            pl.BlockSpec((tt, dm), row),
            pl.BlockSpec((tt, LANES), row),
            pl.BlockSpec((1, dm), lambda i, ds: (0, 0)),
        ],
        out_specs=out_specs,
        scratch_shapes=[pltpu.VMEM((2, TOP_K, tt, dm), F32), pltpu.SemaphoreType.DMA((2,))],
    )
    return pl.pallas_call(
        functools.partial(_combine_kernel, t_tok=tt, last=last),
        out_shape=out_shape,
        grid_spec=grid_spec,
        compiler_params=_cparams(("arbitrary",), 48),
        name="moe_combine",
    )(dest, y, h2, rt, g_next.reshape(1, dm))


def _layer(h, xn, cos, sin, memn, p, d, g_next, last):
    t = d.tokens
    dm = d.d_model
    tm = d.tm
    row128 = pl.BlockSpec((tm, LANES), lambda i, j: (i, 0))

    tn = d.tn_in
    row128_in = pl.BlockSpec((d.tm_in, LANES), lambda i, j: (i, 0))
    z = _matmul(
        xn, p["w_main"], k=dm, a_col=0, tm=d.tm_in, tn=tn, out_dtype=BF16, vmem_mib=60,
        post=functools.partial(_post_in_proj, j_rope0=_exact_div(d.off_qs, tn), j_k=_exact_div(d.off_ks, tn),
                               j_rope1=_exact_div(d.off_vs, tn), tn=tn, q_scale=SWA_HD ** -0.5),
        extras=((cos, row128_in), (sin, row128_in)), name="in_proj")
    kpe = _matmul(xn, p["w_kr"], k=dm, a_col=0, tm=tm, tn=LANES, out_dtype=BF16, post=_post_rope_all,
                  extras=((cos, row128), (sin, row128)), name="rope_key_proj")

    tn_q = min(4, d.mla_heads) * MLA_HEAD_PAD
    q_full = _matmul(
        z, p["w_uq"], k=d.q_lora, a_col=0, tm=tm, tn=tn_q, out_dtype=BF16, pre=_pre_rms,
        post=functools.partial(_post_mla_q, scale=(NOPE_DIM + ROPE_DIM) ** -0.5),
        extras=((p["g_qa"], pl.BlockSpec((1, d.q_lora), lambda i, j: (0, 0))), (cos, row128), (sin, row128)),
        name="mla_q_proj")
    kv = _matmul(
        z, p["w_ukv"], k=d.kv_lora, a_col=_exact_div(d.off_ckv, d.kv_lora), tm=tm, tn=tn_q, out_dtype=BF16,
        pre=_pre_rms, post=_post_cast,
        extras=((p["g_kva"], pl.BlockSpec((1, d.kv_lora), lambda i, j: (0, 0))),), name="mla_kv_proj")
    o_a = _mla_attention(q_full, kv, kpe, d)

    o_b = _swa_attention(z, p["sinks"], d)

    merged = _merge(o_a, p["w_pa"], o_b, p["w_pb"], z, d)
    h1 = _matmul(merged, p["w_o"], k=dm, a_col=0, tm=tm, tn=d.tn, out_dtype=F32, post=_post_residual,
                 extras=((h, pl.BlockSpec((tm, d.tn), lambda i, j: (i, j))),), name="out_proj")

    kvm = _matmul(memn, p["w_xkv"], k=dm, a_col=0, tm=memn.shape[0], tn=d.tn, out_dtype=BF16,
                  post=_post_cast, name="mem_kv_proj")
    h2, xnf, rt, cnt = _cross_router(h1, p["g_cross"], p["w_xq"], kvm, p["w_xo"], p["g_ffn"],
                                     p["w_r"], p["b_r"], d)

    bm = d.moe_block
    counts = cnt[0, :d.n_experts].astype(jnp.int32)
    padded = (counts + bm - 1) // bm * bm
    pad_ends = jnp.cumsum(padded)
    pad_starts = pad_ends - padded
    expert = rt[:, 0:TOP_K].astype(jnp.int32)
    dest = (pad_starts[expert] + rt[:, 4:4 + TOP_K].astype(jnp.int32)).reshape(t * TOP_K)
    fill = jnp.concatenate([jnp.stack([pad_starts + counts, pad_ends], axis=1).reshape(-1),
                            pad_ends[-1:] // bm]).astype(jnp.int32)

    xs = _dispatch(dest, fill, xnf, d)
    y = _experts(_moe_schedule(padded, d), xs, p["w_gate"], p["w_up"], p["w_down"], p["layer"], d)
    return _combine(dest, y, h2, rt, g_next, d, last)


def _prep_layer(l, d, g_mix, w_in, g_qa, g_kva, w_uq, w_ukv, sinks, w_pa, w_pb, w_o, g_cross, w_xq, w_xkv,
                w_xo, g_ffn, w_group, b_group, w_router, b_router, w_gate, w_up, w_down):
    dm = d.d_model
    kr0 = d.q_lora + d.kv_lora
    w = w_in[l]
    w_main = jnp.concatenate([w[:, :kr0].astype(BF16), w[:, kr0 + ROPE_DIM:].astype(BF16)], axis=1)
    w_kr = jnp.pad(w[:, kr0:kr0 + ROPE_DIM], ((0, 0), (0, LANES - ROPE_DIM))).astype(BF16)
    qk = NOPE_DIM + ROPE_DIM
    wq = w_uq[l].reshape(d.q_lora, d.mla_heads, qk)
    wq = jnp.pad(wq, ((0, 0), (0, 0), (0, MLA_HEAD_PAD - qk))).reshape(d.q_lora, d.mla_heads * MLA_HEAD_PAD)
    n_r = d.n_experts + d.n_groups
    w_r = jnp.pad(jnp.concatenate([w_router[l], w_group[l]], axis=1), ((0, 0), (0, LANES - n_r)))
    b_r = jnp.pad(jnp.concatenate([b_router[l], b_group[l]]), (0, LANES - n_r)).reshape(1, LANES)
    return dict(
        w_main=w_main, w_kr=w_kr, w_uq=wq.astype(BF16), w_ukv=w_ukv[l].astype(BF16),
        g_qa=g_qa[l].reshape(1, -1).astype(F32), g_kva=g_kva[l].reshape(1, -1).astype(F32),
        sinks=sinks[l], w_pa=w_pa[l].astype(BF16), w_pb=w_pb[l].astype(BF16), w_o=w_o[l].astype(BF16),
        g_cross=g_cross[l].astype(F32), w_xq=w_xq[l].astype(BF16), w_xkv=w_xkv[l].astype(BF16),
        w_xo=w_xo[l].astype(BF16), g_ffn=g_ffn[l].astype(F32), w_r=w_r.astype(BF16), b_r=b_r.astype(F32),
        w_gate=w_gate, w_up=w_up, w_down=w_down, layer=l,
    )


def _forward(d, x, mem, positions, g_mix, w_in, g_qa, g_kva, w_uq, w_ukv, sinks, w_pa, w_pb, w_o,
             g_cross, g_mem, w_xq, w_xkv, w_xo, g_ffn, w_group, b_group, w_router, b_router,
             w_gate, w_up, w_down, g_final):
    depth = w_in.shape[0]
    t = d.tokens
    dm = d.d_model
    cos, sin = _rope_tables(positions, d)
    memn = _rmsnorm(mem.reshape(d.batch * d.mem_len, dm), g_mem, BF16, d.t_norm)
    h = x.reshape(t, dm)
    xn = _rmsnorm(h, g_mix[0], BF16, d.t_norm)
    for l in range(depth):
        p = _prep_layer(l, d, g_mix, w_in, g_qa, g_kva, w_uq, w_ukv, sinks, w_pa, w_pb, w_o, g_cross,
                        w_xq, w_xkv, w_xo, g_ffn, w_group, b_group, w_router, b_router, w_gate, w_up, w_down)
        last = l == depth - 1
        g_next = g_final if last else g_mix[l + 1]
        res = _layer(h, xn, cos, sin, memn, p, d, g_next, last)
        if last:
            return res.reshape(d.batch, d.seq, dm)
        h, xn = res


def kernel(x, mem, positions, g_mix, w_in, g_qa, g_kva, w_uq, w_ukv, sinks, w_pa, w_pb, w_o, g_cross, g_mem,
           w_xq, w_xkv, w_xo, g_ffn, w_group, b_group, w_router, b_router, w_gate, w_up, w_down, g_final):
    return _forward(Dims(), x, mem, positions, g_mix, w_in, g_qa, g_kva, w_uq, w_ukv, sinks, w_pa, w_pb, w_o,
                    g_cross, g_mem, w_xq, w_xkv, w_xo, g_ffn, w_group, b_group, w_router, b_router,
                    w_gate, w_up, w_down, g_final)
```

```python
import functools
from typing import NamedTuple

import jax
import jax.numpy as jnp
from jax import lax
from jax.experimental import pallas as pl
from jax.experimental.pallas import tpu as pltpu

F32 = jnp.float32
BF16 = jnp.bfloat16
EPS = 1e-6
ROPE_THETA = 10000.0
NEG_INF = -1e30
LANES = 128
ROPE_DIM = 64
NOPE_DIM = 128
MLA_V_DIM = 128
MLA_HEAD_PAD = 256
SWA_HD = 64
X_HD = 128
EPG = 8
TOP_K = 2
MIB = 1024 * 1024


class Dims(NamedTuple):
    batch: int = 4
    seq: int = 2048
    d_model: int = 4096
    mem_len: int = 256
    mla_heads: int = 16
    q_lora: int = 1024
    kv_lora: int = 512
    swa_heads: int = 32
    swa_kv_heads: int = 8
    window: int = 128
    x_heads: int = 4
    n_groups: int = 4
    d_expert: int = 768
    moe_block: int = 128
    tm: int = 1024
    tm_in: int = 2048
    tn_in: int = 512
    tn: int = 512
    tq: int = 512
    t_cross: int = 256
    t_tok: int = 256
    t_norm: int = 256

    @property
    def tokens(self):
        return self.batch * self.seq

    @property
    def n_experts(self):
        return self.n_groups * EPG

    @property
    def swa_q(self):
        return self.swa_heads * SWA_HD

    @property
    def swa_kv(self):
        return self.swa_kv_heads * SWA_HD

    @property
    def off_ckv(self):
        return self.q_lora

    @property
    def off_qs(self):
        return self.q_lora + self.kv_lora

    @property
    def off_ks(self):
        return self.off_qs + self.swa_q

    @property
    def off_vs(self):
        return self.off_ks + self.swa_kv

    @property
    def off_ga(self):
        return self.off_vs + self.swa_kv

    @property
    def off_gb(self):
        return self.off_ga + self.d_model

    @property
    def n_main(self):
        return self.off_gb + self.d_model

    @property
    def n_blocks(self):
        return -(-(self.tokens * TOP_K) // self.moe_block) + self.n_experts


def _exact_div(a, b):
    assert a % b == 0, (a, b)
    return a // b


def _cparams(sem, vmem_mib):
    return pltpu.CompilerParams(dimension_semantics=sem, vmem_limit_bytes=vmem_mib * MIB)


def _rms(x, g):
    return x * lax.rsqrt(jnp.mean(x * x, axis=-1, keepdims=True) + EPS) * g


def _rope128(x, cos, sin):
    lane = lax.broadcasted_iota(jnp.int32, x.shape, 1)
    first_half = (lane % ROPE_DIM) < (ROPE_DIM // 2)
    rot = jnp.where(first_half, -pltpu.roll(x, LANES - ROPE_DIM // 2, 1), pltpu.roll(x, ROPE_DIM // 2, 1))
    return x * cos + rot * sin


def _dot(a, b):
    return jnp.dot(a, b, preferred_element_type=F32)


def _dot_nt(a, b):
    return lax.dot_general(a, b, (((1,), (1,)), ((), ())), preferred_element_type=F32)


def _rope_table_kernel(pos_ref, inv_ref, cos_ref, sin_ref):
    ang = pos_ref[...].astype(F32) * inv_ref[...]
    cos_ref[...] = jnp.cos(ang)
    sin_ref[...] = jnp.sin(ang)


def _rope_tables(positions, d):
    t = d.tokens
    half = ROPE_DIM // 2
    inv_freq = 1.0 / (ROPE_THETA ** (jnp.arange(0, ROPE_DIM, 2, dtype=F32) / ROPE_DIM))
    inv = jnp.tile(inv_freq, LANES // half).reshape(1, LANES)
    tb = min(t, 1024)
    return pl.pallas_call(
        _rope_table_kernel,
        out_shape=(jax.ShapeDtypeStruct((t, LANES), F32),) * 2,
        grid=(_exact_div(t, tb),),
        in_specs=[pl.BlockSpec((tb, 1), lambda i: (i, 0)), pl.BlockSpec((1, LANES), lambda i: (0, 0))],
        out_specs=(pl.BlockSpec((tb, LANES), lambda i: (i, 0)),) * 2,
        name="rope_tables",
    )(positions.reshape(t, 1), inv)


def _rmsnorm_kernel(x_ref, g_ref, o_ref):
    o_ref[...] = _rms(x_ref[...].astype(F32), g_ref[...]).astype(o_ref.dtype)


def _rmsnorm(x, g, out_dtype, tm):
    m, dd = x.shape
    return pl.pallas_call(
        _rmsnorm_kernel,
        out_shape=jax.ShapeDtypeStruct((m, dd), out_dtype),
        grid=(_exact_div(m, tm),),
        in_specs=[pl.BlockSpec((tm, dd), lambda i: (i, 0)), pl.BlockSpec((1, dd), lambda i: (0, 0))],
        out_specs=pl.BlockSpec((tm, dd), lambda i: (i, 0)),
        compiler_params=_cparams(("parallel",), 32),
        name="rmsnorm",
    )(x, g.reshape(1, dd).astype(F32))


def _mm_kernel(*refs, pre, post, n_extra):
    a_ref, w_ref = refs[0], refs[1]
    extras = refs[2:2 + n_extra]
    out_ref = refs[2 + n_extra]
    a = a_ref[...]
    if pre is not None:
        a = pre(a, extras)
    post(_dot(a, w_ref[...]), extras, out_ref)


def _matmul(a, w, *, k, a_col, tm, tn, out_dtype, post, pre=None, extras=(), vmem_mib=48, name):
    m = a.shape[0]
    n = w.shape[1]
    assert w.shape[0] == k
    in_specs = [pl.BlockSpec((tm, k), lambda i, j: (i, a_col)), pl.BlockSpec((k, tn), lambda i, j: (0, j))]
    in_specs += [s for _, s in extras]
    return pl.pallas_call(
        functools.partial(_mm_kernel, pre=pre, post=post, n_extra=len(extras)),
        out_shape=jax.ShapeDtypeStruct((m, n), out_dtype),
        grid=(_exact_div(m, tm), _exact_div(n, tn)),
        in_specs=in_specs,
        out_specs=pl.BlockSpec((tm, tn), lambda i, j: (i, j)),
        compiler_params=_cparams(("parallel", "arbitrary"), vmem_mib),
        name=name,
    )(a, w, *[x for x, _ in extras])


def _post_cast(acc, extras, o_ref):
    o_ref[...] = acc.astype(o_ref.dtype)


def _pre_rms(a, extras):
    return _rms(a.astype(F32), extras[0][...]).astype(BF16)


def _post_residual(acc, extras, o_ref):
    o_ref[...] = extras[0][...] + acc


def _post_in_proj(acc, extras, o_ref, *, j_rope0, j_k, j_rope1, tn, q_scale):
    cos_ref, sin_ref = extras
    j = pl.program_id(1)
    is_rope = (j >= j_rope0) & (j < j_rope1)

    @pl.when(is_rope)
    def _():
        scale = jnp.where(j < j_k, q_scale, 1.0).astype(F32)
        cos = cos_ref[...]
        sin = sin_ref[...]
        for c in range(tn // LANES):
            sl = slice(c * LANES, (c + 1) * LANES)
            o_ref[:, sl] = (_rope128(acc[:, sl], cos, sin) * scale).astype(o_ref.dtype)

    @pl.when(jnp.logical_not(is_rope))
    def _():
        o_ref[...] = acc.astype(o_ref.dtype)


def _post_rope_all(acc, extras, o_ref):
    cos_ref, sin_ref = extras
    cos = cos_ref[...]
    sin = sin_ref[...]
    for c in range(acc.shape[1] // LANES):
        sl = slice(c * LANES, (c + 1) * LANES)
        o_ref[:, sl] = _rope128(acc[:, sl], cos, sin).astype(o_ref.dtype)


def _post_mla_q(acc, extras, o_ref, *, scale):
    _, cos_ref, sin_ref = extras
    cos = cos_ref[...]
    sin = sin_ref[...]
    for hd in range(acc.shape[1] // MLA_HEAD_PAD):
        lo = slice(hd * MLA_HEAD_PAD, hd * MLA_HEAD_PAD + LANES)
        hi = slice(hd * MLA_HEAD_PAD + LANES, (hd + 1) * MLA_HEAD_PAD)
        o_ref[:, lo] = (acc[:, lo] * scale).astype(o_ref.dtype)
        o_ref[:, hi] = (_rope128(acc[:, hi], cos, sin) * scale).astype(o_ref.dtype)


def _merge_kernel(oa_ref, wpa_ref, ob_ref, wpb_ref, ga_ref, gb_ref, o_ref):
    pa = _dot(oa_ref[...], wpa_ref[...])
    pb = _dot(ob_ref[...], wpb_ref[...])
    sa = 1.0 / (1.0 + jnp.exp(-ga_ref[...].astype(F32)))
    sb = 1.0 / (1.0 + jnp.exp(-gb_ref[...].astype(F32)))
    o_ref[...] = (sa * pa + sb * pb).astype(o_ref.dtype)


def _merge(o_a, w_pa, o_b, w_pb, z, d):
    t = d.tokens
    tm, tn = d.tm, d.tn
    ka, kb = o_a.shape[1], o_b.shape[1]
    ja, jb = _exact_div(d.off_ga, tn), _exact_div(d.off_gb, tn)
    return pl.pallas_call(
        _merge_kernel,
        out_shape=jax.ShapeDtypeStruct((t, d.d_model), BF16),
        grid=(_exact_div(t, tm), _exact_div(d.d_model, tn)),
        in_specs=[
            pl.BlockSpec((tm, ka), lambda i, j: (i, 0)),
            pl.BlockSpec((ka, tn), lambda i, j: (0, j)),
            pl.BlockSpec((tm, kb), lambda i, j: (i, 0)),
            pl.BlockSpec((kb, tn), lambda i, j: (0, j)),
            pl.BlockSpec((tm, tn), lambda i, j: (i, j + ja)),
            pl.BlockSpec((tm, tn), lambda i, j: (i, j + jb)),
        ],
        out_specs=pl.BlockSpec((tm, tn), lambda i, j: (i, j)),
        compiler_params=_cparams(("parallel", "arbitrary"), 48),
        name="gated_merge",
    )(o_a, w_pa, o_b, w_pb, z, z)


def _mla_kernel(q_ref, kv_ref, kpe_ref, o_ref, k_scr, *, seq, tq):
    k_scr[:, :NOPE_DIM] = kv_ref[:, :NOPE_DIM]
    k_scr[:, NOPE_DIM:] = kpe_ref[...]
    for i in range(seq // tq):
        ln = (i + 1) * tq
        q = q_ref[i * tq:(i + 1) * tq, :]
        s = _dot_nt(q, k_scr[:ln, :])
        row = lax.broadcasted_iota(jnp.int32, (tq, ln), 0) + i * tq
        col = lax.broadcasted_iota(jnp.int32, (tq, ln), 1)
        s = jnp.where(col <= row, s, NEG_INF)
        m = jnp.max(s, axis=-1, keepdims=True)
        p = jnp.exp(s - m)
        l = jnp.sum(p, axis=-1, keepdims=True)
        o = _dot(p.astype(BF16), kv_ref[:ln, NOPE_DIM:])
        o_ref[i * tq:(i + 1) * tq, :] = (o / l).astype(o_ref.dtype)


def _mla_attention(q_full, kv, kpe, d):
    t = d.tokens
    return pl.pallas_call(
        functools.partial(_mla_kernel, seq=d.seq, tq=d.tq),
        out_shape=jax.ShapeDtypeStruct((t, d.mla_heads * MLA_V_DIM), BF16),
        grid=(d.batch, d.mla_heads),
        in_specs=[
            pl.BlockSpec((d.seq, MLA_HEAD_PAD), lambda b, h: (b, h)),
            pl.BlockSpec((d.seq, NOPE_DIM + MLA_V_DIM), lambda b, h: (b, h)),
            pl.BlockSpec((d.seq, LANES), lambda b, h: (b, 0)),
        ],
        out_specs=pl.BlockSpec((d.seq, MLA_V_DIM), lambda b, h: (b, h)),
        scratch_shapes=[pltpu.VMEM((d.seq, MLA_HEAD_PAD), BF16)],
        compiler_params=_cparams(("parallel", "parallel"), 48),
        name="mla_attention",
    )(q_full, kv, kpe)


def _swa_kernel(sink_ref, q_ref, k_ref, v_ref, o_ref, klo, khi, vlo, vhi, *, seq, window, heads_per_step):
    pair = pl.program_id(1)
    w = window
    lane = lax.broadcasted_iota(jnp.int32, (seq, LANES), 1)
    low = lane < SWA_HD
    for src_ref, lo_ref, hi_ref in ((k_ref, klo, khi), (v_ref, vlo, vhi)):
        x = src_ref[...].astype(F32)
        xs = pltpu.roll(x, SWA_HD, 1)
        zero = jnp.zeros_like(x)
        lo_ref[0] = jnp.where(low, x, zero).astype(BF16)
        hi_ref[0] = jnp.where(low, zero, xs).astype(BF16)
        lo_ref[1] = jnp.where(low, xs, zero).astype(BF16)
        hi_ref[1] = jnp.where(low, zero, x).astype(BF16)

    n_tiles = heads_per_step * SWA_HD // LANES
    tiles_per_kv = n_tiles // 2

    m_rows = tiles_per_kv * w
    row = lax.broadcasted_iota(jnp.int32, (m_rows, 1), 0)
    out_lane = lax.broadcasted_iota(jnp.int32, (m_rows, LANES), 1)

    def mask_bias(klen, is_first):
        qq = lax.broadcasted_iota(jnp.int32, (m_rows, klen), 0) % w
        kk = lax.broadcasted_iota(jnp.int32, (m_rows, klen), 1)
        valid = (kk <= qq) if is_first else ((kk > qq) & (kk <= qq + w))
        return jnp.where(valid, 0.0, NEG_INF).astype(F32)

    def sink_column(g, half):
        sink = jnp.zeros((m_rows, 1), F32)
        for ti in range(tiles_per_kv):
            head = pair * heads_per_step + 2 * (g * tiles_per_kv + ti) + half
            sink = jnp.where((row >= ti * w) & (row < (ti + 1) * w), sink_ref[head], sink)
        return sink

    sinks = [[sink_column(g, half) for half in range(2)] for g in range(2)]

    def block(r0, k0, klen, bias):
        for g in range(2):
            tiles = range(g * tiles_per_kv, (g + 1) * tiles_per_kv)
            q = jnp.concatenate([q_ref[pl.ds(r0, w), c * LANES:(c + 1) * LANES] for c in tiles], axis=0)
            kcat = jnp.concatenate([klo[g, pl.ds(k0, klen), :], khi[g, pl.ds(k0, klen), :]], axis=0)
            vcat = jnp.concatenate([vlo[g, pl.ds(k0, klen), :], vhi[g, pl.ds(k0, klen), :]], axis=0)
            s = _dot_nt(q, kcat)
            probs, inv_den = [], []
            for half in range(2):
                sink = sinks[g][half]
                sh = s[:, half * klen:(half + 1) * klen] + bias
                m = jnp.maximum(jnp.max(sh, axis=-1, keepdims=True), sink)
                p = jnp.exp(sh - m)
                inv_den.append(1.0 / (jnp.sum(p, axis=-1, keepdims=True) + jnp.exp(sink - m)))
                probs.append(p.astype(BF16))
            o = _dot(jnp.concatenate(probs, axis=1), vcat) * jnp.where(out_lane < SWA_HD, inv_den[0], inv_den[1])
            for ti, c in enumerate(tiles):
                o_ref[pl.ds(r0, w), c * LANES:(c + 1) * LANES] = o[ti * w:(ti + 1) * w].astype(o_ref.dtype)

    block(0, 0, w, mask_bias(w, True))
    band = mask_bias(2 * w, False)

    def body(n, carry):
        r0 = pl.multiple_of(n * w, w)
        block(r0, pl.multiple_of(r0 - w, w), 2 * w, band)
        return carry

    n_blk = seq // w
    lax.fori_loop(1, n_blk, body, 0, unroll=3 if (n_blk - 1) % 3 == 0 else 1)


def _swa_attention(z, sinks, d):
    t = d.tokens
    hps = 2 * (d.swa_heads // d.swa_kv_heads)
    qw = hps * SWA_HD
    n_pairs = _exact_div(d.swa_kv_heads, 2)
    jq, jk, jv = _exact_div(d.off_qs, qw), _exact_div(d.off_ks, LANES), _exact_div(d.off_vs, LANES)
    grid_spec = pltpu.PrefetchScalarGridSpec(
        num_scalar_prefetch=1,
        grid=(d.batch, n_pairs),
        in_specs=[
            pl.BlockSpec((d.seq, qw), lambda b, p, s: (b, jq + p)),
            pl.BlockSpec((d.seq, LANES), lambda b, p, s: (b, jk + p)),
            pl.BlockSpec((d.seq, LANES), lambda b, p, s: (b, jv + p)),
        ],
        out_specs=pl.BlockSpec((d.seq, qw), lambda b, p, s: (b, p)),
        scratch_shapes=[pltpu.VMEM((2, d.seq, LANES), BF16)] * 4,
    )
    return pl.pallas_call(
        functools.partial(_swa_kernel, seq=d.seq, window=d.window, heads_per_step=hps),
        out_shape=jax.ShapeDtypeStruct((t, d.swa_q), BF16),
        grid_spec=grid_spec,
        compiler_params=_cparams(("parallel", "parallel"), 48),
        name="swa_attention",
    )(sinks.astype(F32), z, z, z)


def _cross_router_kernel(h_ref, gc_ref, wq_ref, kvm_ref, wo_ref, gf_ref, wr_ref, br_ref,
                         h2_ref, xnf_ref, rt_ref, cnt_ref, run_ref, *, x_heads, n_groups, scale):
    tm = h_ref.shape[0]
    hx = x_heads * X_HD
    n_exp = n_groups * EPG

    @pl.when(pl.program_id(0) == 0)
    def _():
        run_ref[...] = jnp.zeros_like(run_ref)

    h = h_ref[...]
    hn = _rms(h, gc_ref[...]).astype(BF16)
    q = (_dot(hn, wq_ref[...]) * scale).astype(BF16)
    outs = []
    for hd in range(x_heads):
        kh = kvm_ref[:, hd * X_HD:(hd + 1) * X_HD]
        vh = kvm_ref[:, hx + hd * X_HD:hx + (hd + 1) * X_HD]
        s = _dot_nt(q[:, hd * X_HD:(hd + 1) * X_HD], kh)
        m = jnp.max(s, axis=-1, keepdims=True)
        p = jnp.exp(s - m)
        l = jnp.sum(p, axis=-1, keepdims=True)
        outs.append((_dot(p.astype(BF16), vh) / l).astype(BF16))
    h2 = h + _dot(jnp.concatenate(outs, axis=1), wo_ref[...])
    h2_ref[...] = h2
    xnf = _rms(h2, gf_ref[...])
    xnf_ref[...] = xnf

    logits = _dot(xnf.astype(BF16), wr_ref[...]) + br_ref[...]
    lane = lax.broadcasted_iota(jnp.int32, (tm, LANES), 1).astype(F32)
    big = float(LANES)
    is_group = (lane >= n_exp) & (lane < n_exp + n_groups)
    gl = jnp.where(is_group, logits, -jnp.inf)
    gmax = jnp.max(gl, axis=-1, keepdims=True)
    g_lane = jnp.min(jnp.where(gl == gmax, lane, big), axis=-1, keepdims=True)
    p_group = 1.0 / jnp.sum(jnp.where(is_group, jnp.exp(gl - gmax), 0.0), axis=-1, keepdims=True)
    e_lo = (g_lane - n_exp) * EPG
    in_group = (lane >= e_lo) & (lane < e_lo + EPG)
    el = jnp.where(in_group, logits, -jnp.inf)
    m1 = jnp.max(el, axis=-1, keepdims=True)
    i1 = jnp.min(jnp.where(el == m1, lane, big), axis=-1, keepdims=True)
    el2 = jnp.where(lane == i1, -jnp.inf, el)
    m2 = jnp.max(el2, axis=-1, keepdims=True)
    i2 = jnp.min(jnp.where(el2 == m2, lane, big), axis=-1, keepdims=True)
    w2 = jnp.exp(m2 - m1)
    gate1 = p_group / (1.0 + w2)
    gate2 = gate1 * w2

    hot1 = lane == i1
    hot2 = lane == i2
    onehot = jnp.where(hot1 | hot2, 1.0, 0.0)
    rr = lax.broadcasted_iota(jnp.int32, (tm, tm), 0)
    cc = lax.broadcasted_iota(jnp.int32, (tm, tm), 1)
    tri = jnp.where(cc < rr, 1.0, 0.0).astype(BF16)
    before = _dot(tri, onehot.astype(BF16)) + run_ref[...]
    rank1 = jnp.sum(jnp.where(hot1, before, 0.0), axis=-1, keepdims=True)
    rank2 = jnp.sum(jnp.where(hot2, before, 0.0), axis=-1, keepdims=True)
    run = run_ref[...] + jnp.sum(onehot, axis=0, keepdims=True)
    run_ref[...] = run
    cnt_ref[...] = run

    rt = jnp.where(lane == 0, i1, 0.0)
    rt = jnp.where(lane == 1, i2, rt)
    rt = jnp.where(lane == 2, gate1, rt)
    rt = jnp.where(lane == 3, gate2, rt)
    rt = jnp.where(lane == 4, rank1, rt)
    rt = jnp.where(lane == 5, rank2, rt)
    rt_ref[...] = rt


def _cross_router(h, g_cross, w_xq, kvm, w_xo, g_ffn, w_r, b_r, d):
    t = d.tokens
    tm = d.t_cross
    dm = d.d_model
    hx = d.x_heads * X_HD
    steps_per_batch = _exact_div(d.seq, tm)
    row = lambda i: (i, 0)
    fixed = lambda i: (0, 0)
    return pl.pallas_call(
        functools.partial(_cross_router_kernel, x_heads=d.x_heads, n_groups=d.n_groups, scale=X_HD ** -0.5),
        out_shape=(
            jax.ShapeDtypeStruct((t, dm), F32),
            jax.ShapeDtypeStruct((t, dm), F32),
            jax.ShapeDtypeStruct((t, LANES), F32),
            jax.ShapeDtypeStruct((1, LANES), F32),
        ),
        grid=(_exact_div(t, tm),),
        in_specs=[
            pl.BlockSpec((tm, dm), row),
            pl.BlockSpec((1, dm), fixed),
            pl.BlockSpec((dm, hx), fixed, pipeline_mode=pl.Buffered(1)),
            pl.BlockSpec((d.mem_len, 2 * hx), lambda i: (i // steps_per_batch, 0)),
            pl.BlockSpec((hx, dm), fixed, pipeline_mode=pl.Buffered(1)),
            pl.BlockSpec((1, dm), fixed),
            pl.BlockSpec((dm, LANES), fixed, pipeline_mode=pl.Buffered(1)),
            pl.BlockSpec((1, LANES), fixed),
        ],
        out_specs=(
            pl.BlockSpec((tm, dm), row),
            pl.BlockSpec((tm, dm), row),
            pl.BlockSpec((tm, LANES), row),
            pl.BlockSpec((1, LANES), fixed),
        ),
        scratch_shapes=[pltpu.VMEM((1, LANES), F32)],
        compiler_params=_cparams(("arbitrary",), 56),
        name="cross_attention_router",
    )(h, g_cross.reshape(1, dm), w_xq, kvm, w_xo, g_ffn.reshape(1, dm), w_r, b_r)


def _row_copy(src_hbm, src_row, dst_ref, dst_row, sem):
    return pltpu.make_async_copy(src_hbm.at[pl.ds(src_row, 1)], dst_ref.at[pl.ds(dst_row, 1)], sem)


W_SLABS = 4


def _moe_steps(d):
    return W_SLABS + d.n_blocks + W_SLABS * d.n_experts + 1


def _moe_schedule(padded, d):
    bm, ne, nblk = d.moe_block, d.n_experts, d.n_blocks
    i32 = jnp.int32
    nb = padded // bm
    first_blk = (jnp.cumsum(padded) - padded) // bm
    n_used = jnp.sum(nb)
    per_expert = jnp.where(jnp.arange(ne) == ne - 1, nb, jnp.maximum(nb, W_SLABS))
    step_end = W_SLABS + jnp.cumsum(per_expert)
    step_start = step_end - per_expert
    total = step_end[-1]
    s = jnp.arange(_moe_steps(d), dtype=i32)
    is_pro = s < W_SLABS
    is_tail = s >= total
    e = jnp.minimum(jnp.sum((step_end[None, :] <= s[:, None]).astype(i32), axis=1), ne - 1)
    j = jnp.where(is_pro, s, s - step_start[e])
    has_blk = jnp.logical_not(is_pro | is_tail) & (j < nb[e])
    blk = jnp.where(is_pro, 0, first_blk[e] + jnp.minimum(j, nb[e]))
    xblk = jnp.where(has_blk, blk, 0)
    oblk = jnp.where(is_tail, jnp.minimum(n_used + s - total, nblk), blk)
    nxt = jnp.where(is_pro, 0, jnp.minimum(e + 1, ne - 1))
    slab = jnp.where(is_tail, W_SLABS - 1, jnp.minimum(j, W_SLABS - 1))
    slot = jnp.where(is_pro, 1, e % 2)
    return tuple(v.astype(i32) for v in (xblk, oblk, nxt, slab, slot, is_tail)), has_blk.astype(i32)


def _silu_mul(a, b):
    return (a / (1.0 + jnp.exp(-a))) * b


def _expert_up_kernel(xblk, oblk, nxt, slab, slot, tail, has_blk, has_next, slot_tok, x_hbm, wg_st, wu_st, hb_ref,
                      wg0, wg1, wu0, wu1, xbuf, xsem, *, bm):
    del oblk, nxt
    s = pl.program_id(0)
    buf = s % 2
    rows = wg_st.shape[1]

    @pl.when(s == 0)
    def _():
        wg1[...] = jnp.zeros_like(wg1)
        wu1[...] = jnp.zeros_like(wu1)
        xbuf[...] = jnp.zeros_like(xbuf)

    @pl.when(has_next[s] == 1)
    def _():
        base = xblk[s + 1] * bm

        def issue(r, carry):
            _row_copy(x_hbm, slot_tok[base + r], xbuf.at[1 - buf], r, xsem.at[1 - buf]).start()
            return carry

        lax.fori_loop(0, bm, issue, 0)

    @pl.when(has_blk[s] == 1)
    def _():
        def drain(r, carry):
            _row_copy(x_hbm, 0, xbuf.at[buf], 0, xsem.at[buf]).wait()
            return carry

        lax.fori_loop(0, bm, drain, 0)

    @pl.when(tail[s] == 1)
    def _():
        hb_ref[...] = jnp.zeros_like(hb_ref)

    r0 = pl.multiple_of(slab[s] * rows, rows)
    for cur, (wg_c, wu_c, wg_n, wu_n) in enumerate(((wg0, wu0, wg1, wu1), (wg1, wu1, wg0, wu0))):
        @pl.when((tail[s] == 0) & (slot[s] == cur))
        def _():
            wg_n[pl.ds(r0, rows), :] = wg_st[0].astype(BF16)
            wu_n[pl.ds(r0, rows), :] = wu_st[0].astype(BF16)
            x = xbuf[buf].astype(BF16)
            hb_ref[...] = _silu_mul(_dot(x, wg_c[...]), _dot(x, wu_c[...])).astype(hb_ref.dtype)


def _expert_down_kernel(xblk, oblk, nxt, slab, slot, tail, hb_ref, wd_st, y_ref, wd0, wd1):
    del xblk, oblk, nxt
    s = pl.program_id(0)
    rows = wd_st.shape[1]

    @pl.when(s == 0)
    def _():
        wd1[...] = jnp.zeros_like(wd1)

    @pl.when(tail[s] == 1)
    def _():
        y_ref[...] = jnp.zeros_like(y_ref)

    r0 = pl.multiple_of(slab[s] * rows, rows)
    for cur, (wd_c, wd_n) in enumerate(((wd0, wd1), (wd1, wd0))):
        @pl.when((tail[s] == 0) & (slot[s] == cur))
        def _():
            wd_n[pl.ds(r0, rows), :] = wd_st[0].astype(BF16)
            y_ref[...] = _dot(hb_ref[...], wd_c[...])


def _experts(schedule, slot_tok, x, w_gate, w_up, w_down, layer, d):
    sched, has_blk = schedule
    has_next = jnp.concatenate([has_blk[1:], jnp.zeros((1,), jnp.int32)])
    up_extra = (has_blk, has_next, slot_tok)
    wide = lambda f: (lambda s, xb, ob, nx, sl, st, tl, hb, hn, tk: f(s, xb, ob, nx, sl, st, tl))
    bm = d.moe_block
    dm, de = d.d_model, d.d_expert
    ne = d.n_experts
    rows_out = (d.n_blocks + 1) * bm
    w_gate, w_up = w_gate.reshape(-1, dm, de), w_up.reshape(-1, dm, de)
    w_down = w_down.reshape(-1, de, dm)
    x_map = lambda s, xb, ob, nx, sl, st, tl: (xb[s], 0)
    o_map = lambda s, xb, ob, nx, sl, st, tl: (ob[s], 0)
    w_map = lambda s, xb, ob, nx, sl, st, tl: (layer * ne + nx[s], sl[s], 0)
    ku, kd = _exact_div(dm, W_SLABS), _exact_div(de, W_SLABS)
    hb = pl.pallas_call(
        functools.partial(_expert_up_kernel, bm=bm),
        out_shape=jax.ShapeDtypeStruct((rows_out, de), BF16),
        grid_spec=pltpu.PrefetchScalarGridSpec(
            num_scalar_prefetch=len(sched) + len(up_extra),
            grid=(_moe_steps(d),),
            in_specs=[pl.BlockSpec(memory_space=pl.ANY), pl.BlockSpec((1, ku, de), wide(w_map)),
                      pl.BlockSpec((1, ku, de), wide(w_map))],
            out_specs=pl.BlockSpec((bm, de), wide(o_map)),
            scratch_shapes=[pltpu.VMEM((dm, de), BF16)] * 4
            + [pltpu.VMEM((2, bm, dm), x.dtype), pltpu.SemaphoreType.DMA((2,))],
        ),
        compiler_params=_cparams(("arbitrary",), 56),
        name="moe_experts_up",
    )(*sched, *up_extra, x, w_gate, w_up)
    return pl.pallas_call(
        _expert_down_kernel,
        out_shape=jax.ShapeDtypeStruct((rows_out, dm), F32),
        grid_spec=pltpu.PrefetchScalarGridSpec(
            num_scalar_prefetch=len(sched),
            grid=(_moe_steps(d),),
            in_specs=[pl.BlockSpec((bm, de), x_map), pl.BlockSpec((1, kd, dm), w_map)],
            out_specs=pl.BlockSpec((bm, dm), o_map),
            scratch_shapes=[pltpu.VMEM((de, dm), BF16)] * 2,
        ),
        compiler_params=_cparams(("arbitrary",), 40),
        name="moe_experts_down",
    )(*sched, hb, w_down)


def _combine_kernel(dest_ref, y_hbm, h_ref, rt_ref, g_ref, *rest, t_tok, last):
    if last:
        out_ref, ybuf, sem = rest
    else:
        h3_ref, out_ref, ybuf, sem = rest
    i = pl.program_id(0)
    buf = i % 2

    def gather(step, b):
        base = step * t_tok

        def issue(r, carry):
            for k in range(TOP_K):
                _row_copy(y_hbm, dest_ref[TOP_K * (base + r) + k], ybuf.at[b, k], r, sem.at[b]).start()
            return carry

        lax.fori_loop(0, t_tok, issue, 0)

    @pl.when(i == 0)
    def _():
        gather(0, 0)

    @pl.when(i + 1 < pl.num_programs(0))
    def _():
        gather(i + 1, 1 - buf)

    def drain(r, carry):
        for k in range(TOP_K):
            _row_copy(y_hbm, 0, ybuf.at[buf, k], 0, sem.at[buf]).wait()
        return carry

    lax.fori_loop(0, t_tok, drain, 0)
    rt = rt_ref[...]
    h3 = h_ref[...] + rt[:, 2:3] * ybuf[buf, 0] + rt[:, 3:4] * ybuf[buf, 1]
    if not last:
        h3_ref[...] = h3
    out_ref[...] = _rms(h3, g_ref[...]).astype(out_ref.dtype)


def _combine(dest, y, h2, rt, g_next, d, last):
    t = d.tokens
    tt = d.t_tok
    dm = d.d_model
    row = lambda i, ds: (i, 0)
    out_specs = pl.BlockSpec((tt, dm), row)
    if last:
        out_shape = jax.ShapeDtypeStruct((t, dm), F32)
    else:
        out_shape = (jax.ShapeDtypeStruct((t, dm), F32), jax.ShapeDtypeStruct((t, dm), BF16))
        out_specs = (out_specs, pl.BlockSpec((tt, dm), row))
    grid_spec = pltpu.PrefetchScalarGridSpec(
        num_scalar_prefetch=1,
        grid=(_exact_div(t, tt),),
        in_specs=[
            pl.BlockSpec(memory_space=pl.ANY),
            pl.BlockSpec((tt, dm), row),
            pl.BlockSpec((tt, LANES), row),
            pl.BlockSpec((1, dm), lambda i, ds: (0, 0)),
        ],
        out_specs=out_specs,
        scratch_shapes=[pltpu.VMEM((2, TOP_K, tt, dm), F32), pltpu.SemaphoreType.DMA((2,))],
    )
    return pl.pallas_call(
        functools.partial(_combine_kernel, t_tok=tt, last=last),
        out_shape=out_shape,
        grid_spec=grid_spec,
        compiler_params=_cparams(("arbitrary",), 48),
        name="moe_combine",
    )(dest, y, h2, rt, g_next.reshape(1, dm))


def _layer(h, xn, cos, sin, memn, p, d, g_next, last):
    t = d.tokens
    dm = d.d_model
    tm = d.tm
    row128 = pl.BlockSpec((tm, LANES), lambda i, j: (i, 0))

    tn = d.tn_in
    row128_in = pl.BlockSpec((d.tm_in, LANES), lambda i, j: (i, 0))
    z = _matmul(
        xn, p["w_main"], k=dm, a_col=0, tm=d.tm_in, tn=tn, out_dtype=BF16, vmem_mib=60,
        post=functools.partial(_post_in_proj, j_rope0=_exact_div(d.off_qs, tn), j_k=_exact_div(d.off_ks, tn),
                               j_rope1=_exact_div(d.off_vs, tn), tn=tn, q_scale=SWA_HD ** -0.5),
        extras=((cos, row128_in), (sin, row128_in)), name="in_proj")
    kpe = _matmul(xn, p["w_kr"], k=dm, a_col=0, tm=tm, tn=LANES, out_dtype=BF16, post=_post_rope_all,
                  extras=((cos, row128), (sin, row128)), name="rope_key_proj")

    tn_q = min(4, d.mla_heads) * MLA_HEAD_PAD
    q_full = _matmul(
        z, p["w_uq"], k=d.q_lora, a_col=0, tm=tm, tn=tn_q, out_dtype=BF16, pre=_pre_rms,
        post=functools.partial(_post_mla_q, scale=(NOPE_DIM + ROPE_DIM) ** -0.5),
        extras=((p["g_qa"], pl.BlockSpec((1, d.q_lora), lambda i, j: (0, 0))), (cos, row128), (sin, row128)),
        name="mla_q_proj")
    kv = _matmul(
        z, p["w_ukv"], k=d.kv_lora, a_col=_exact_div(d.off_ckv, d.kv_lora), tm=tm, tn=tn_q, out_dtype=BF16,
        pre=_pre_rms, post=_post_cast,
        extras=((p["g_kva"], pl.BlockSpec((1, d.kv_lora), lambda i, j: (0, 0))),), name="mla_kv_proj")
    o_a = _mla_attention(q_full, kv, kpe, d)

    o_b = _swa_attention(z, p["sinks"], d)

    merged = _merge(o_a, p["w_pa"], o_b, p["w_pb"], z, d)
    h1 = _matmul(merged, p["w_o"], k=dm, a_col=0, tm=tm, tn=d.tn, out_dtype=F32, post=_post_residual,
                 extras=((h, pl.BlockSpec((tm, d.tn), lambda i, j: (i, j))),), name="out_proj")

    kvm = _matmul(memn, p["w_xkv"], k=dm, a_col=0, tm=memn.shape[0], tn=d.tn, out_dtype=BF16,
                  post=_post_cast, name="mem_kv_proj")
    h2, xnf, rt, cnt = _cross_router(h1, p["g_cross"], p["w_xq"], kvm, p["w_xo"], p["g_ffn"],
                                     p["w_r"], p["b_r"], d)

    bm = d.moe_block
    counts = cnt[0, :d.n_experts].astype(jnp.int32)
    padded = (counts + bm - 1) // bm * bm
    pad_ends = jnp.cumsum(padded)
    pad_starts = pad_ends - padded
    expert = rt[:, 0:TOP_K].astype(jnp.int32)
    dest = (pad_starts[expert] + rt[:, 4:4 + TOP_K].astype(jnp.int32)).reshape(t * TOP_K)
    tok = jnp.arange(t * TOP_K, dtype=jnp.int32) // TOP_K
    slot_tok = jnp.zeros((d.n_blocks * bm,), jnp.int32).at[dest].set(tok)

    y = _experts(_moe_schedule(padded, d), slot_tok, xnf, p["w_gate"], p["w_up"], p["w_down"], p["layer"], d)
    return _combine(dest, y, h2, rt, g_next, d, last)


def _prep_layer(l, d, g_mix, w_in, g_qa, g_kva, w_uq, w_ukv, sinks, w_pa, w_pb, w_o, g_cross, w_xq, w_xkv,
                w_xo, g_ffn, w_group, b_group, w_router, b_router, w_gate, w_up, w_down):
    dm = d.d_model
    kr0 = d.q_lora + d.kv_lora
    w = w_in[l]
    w_main = jnp.concatenate([w[:, :kr0].astype(BF16), w[:, kr0 + ROPE_DIM:].astype(BF16)], axis=1)
    w_kr = jnp.pad(w[:, kr0:kr0 + ROPE_DIM], ((0, 0), (0, LANES - ROPE_DIM))).astype(BF16)
    qk = NOPE_DIM + ROPE_DIM
    wq = w_uq[l].reshape(d.q_lora, d.mla_heads, qk)
    wq = jnp.pad(wq, ((0, 0), (0, 0), (0, MLA_HEAD_PAD - qk))).reshape(d.q_lora, d.mla_heads * MLA_HEAD_PAD)
    n_r = d.n_experts + d.n_groups
    w_r = jnp.pad(jnp.concatenate([w_router[l], w_group[l]], axis=1), ((0, 0), (0, LANES - n_r)))
    b_r = jnp.pad(jnp.concatenate([b_router[l], b_group[l]]), (0, LANES - n_r)).reshape(1, LANES)
    return dict(
        w_main=w_main, w_kr=w_kr, w_uq=wq.astype(BF16), w_ukv=w_ukv[l].astype(BF16),
        g_qa=g_qa[l].reshape(1, -1).astype(F32), g_kva=g_kva[l].reshape(1, -1).astype(F32),
        sinks=sinks[l], w_pa=w_pa[l].astype(BF16), w_pb=w_pb[l].astype(BF16), w_o=w_o[l].astype(BF16),
        g_cross=g_cross[l].astype(F32), w_xq=w_xq[l].astype(BF16), w_xkv=w_xkv[l].astype(BF16),
        w_xo=w_xo[l].astype(BF16), g_ffn=g_ffn[l].astype(F32), w_r=w_r.astype(BF16), b_r=b_r.astype(F32),
        w_gate=w_gate, w_up=w_up, w_down=w_down, layer=l,
    )


def _forward(d, x, mem, positions, g_mix, w_in, g_qa, g_kva, w_uq, w_ukv, sinks, w_pa, w_pb, w_o,
             g_cross, g_mem, w_xq, w_xkv, w_xo, g_ffn, w_group, b_group, w_router, b_router,
             w_gate, w_up, w_down, g_final):
    depth = w_in.shape[0]
    t = d.tokens
    dm = d.d_model
    cos, sin = _rope_tables(positions, d)
    memn = _rmsnorm(mem.reshape(d.batch * d.mem_len, dm), g_mem, BF16, d.t_norm)
    h = x.reshape(t, dm)
    xn = _rmsnorm(h, g_mix[0], BF16, d.t_norm)
    for l in range(depth):
        p = _prep_layer(l, d, g_mix, w_in, g_qa, g_kva, w_uq, w_ukv, sinks, w_pa, w_pb, w_o, g_cross,
                        w_xq, w_xkv, w_xo, g_ffn, w_group, b_group, w_router, b_router, w_gate, w_up, w_down)
        last = l == depth - 1
        g_next = g_final if last else g_mix[l + 1]
        res = _layer(h, xn, cos, sin, memn, p, d, g_next, last)
        if last:
            return res.reshape(d.batch, d.seq, dm)
        h, xn = res


def kernel(x, mem, positions, g_mix, w_in, g_qa, g_kva, w_uq, w_ukv, sinks, w_pa, w_pb, w_o, g_cross, g_mem,
           w_xq, w_xkv, w_xo, g_ffn, w_group, b_group, w_router, b_router, w_gate, w_up, w_down, g_final):
    return _forward(Dims(), x, mem, positions, g_mix, w_in, g_qa, g_kva, w_uq, w_ukv, sinks, w_pa, w_pb, w_o,
                    g_cross, g_mem, w_xq, w_xkv, w_xo, g_ffn, w_group, b_group, w_router, b_router,
                    w_gate, w_up, w_down, g_final)
```

```python
import functools
from typing import NamedTuple

import jax
import jax.numpy as jnp
from jax import lax
from jax.experimental import pallas as pl
from jax.experimental.pallas import tpu as pltpu

F32 = jnp.float32
BF16 = jnp.bfloat16
EPS = 1e-6
ROPE_THETA = 10000.0
NEG_INF = -1e30
LANES = 128
ROPE_DIM = 64
NOPE_DIM = 128
MLA_V_DIM = 128
MLA_HEAD_PAD = 256
SWA_HD = 64
X_HD = 128
EPG = 8
TOP_K = 2
MIB = 1024 * 1024


class Dims(NamedTuple):
    batch: int = 4
    seq: int = 2048
    d_model: int = 4096
    mem_len: int = 256
    mla_heads: int = 16
    q_lora: int = 1024
    kv_lora: int = 512
    swa_heads: int = 32
    swa_kv_heads: int = 8
    window: int = 128
    x_heads: int = 4
    n_groups: int = 4
    d_expert: int = 768
    moe_block: int = 128
    tm: int = 1024
    tm_in: int = 2048
    tn_in: int = 512
    tn: int = 512
    tq: int = 512
    t_cross: int = 256
    t_tok: int = 256
    t_norm: int = 256

    @property
    def tokens(self):
        return self.batch * self.seq

    @property
    def n_experts(self):
        return self.n_groups * EPG

    @property
    def swa_q(self):
        return self.swa_heads * SWA_HD

    @property
    def swa_kv(self):
        return self.swa_kv_heads * SWA_HD

    @property
    def off_ckv(self):
        return self.q_lora

    @property
    def off_qs(self):
        return self.q_lora + self.kv_lora

    @property
    def off_ks(self):
        return self.off_qs + self.swa_q

    @property
    def off_vs(self):
        return self.off_ks + self.swa_kv

    @property
    def off_ga(self):
        return self.off_vs + self.swa_kv

    @property
    def off_gb(self):
        return self.off_ga + self.d_model

    @property
    def n_main(self):
        return self.off_gb + self.d_model

    @property
    def n_blocks(self):
        return -(-(self.tokens * TOP_K) // self.moe_block) + self.n_experts


def _exact_div(a, b):
    assert a % b == 0, (a, b)
    return a // b


def _cparams(sem, vmem_mib):
    return pltpu.CompilerParams(dimension_semantics=sem, vmem_limit_bytes=vmem_mib * MIB)


def _rms(x, g):
    return x * lax.rsqrt(jnp.mean(x * x, axis=-1, keepdims=True) + EPS) * g


def _rope128(x, cos, sin):
    lane = lax.broadcasted_iota(jnp.int32, x.shape, 1)
    first_half = (lane % ROPE_DIM) < (ROPE_DIM // 2)
    rot = jnp.where(first_half, -pltpu.roll(x, LANES - ROPE_DIM // 2, 1), pltpu.roll(x, ROPE_DIM // 2, 1))
    return x * cos + rot * sin


def _dot(a, b):
    return jnp.dot(a, b, preferred_element_type=F32)


def _dot_nt(a, b):
    return lax.dot_general(a, b, (((1,), (1,)), ((), ())), preferred_element_type=F32)


def _rope_table_kernel(pos_ref, inv_ref, cos_ref, sin_ref):
    ang = pos_ref[...].astype(F32) * inv_ref[...]
    cos_ref[...] = jnp.cos(ang)
    sin_ref[...] = jnp.sin(ang)


def _rope_tables(positions, d):
    t = d.tokens
    half = ROPE_DIM // 2
    inv_freq = 1.0 / (ROPE_THETA ** (jnp.arange(0, ROPE_DIM, 2, dtype=F32) / ROPE_DIM))
    inv = jnp.tile(inv_freq, LANES // half).reshape(1, LANES)
    tb = min(t, 1024)
    return pl.pallas_call(
        _rope_table_kernel,
        out_shape=(jax.ShapeDtypeStruct((t, LANES), F32),) * 2,
        grid=(_exact_div(t, tb),),
        in_specs=[pl.BlockSpec((tb, 1), lambda i: (i, 0)), pl.BlockSpec((1, LANES), lambda i: (0, 0))],
        out_specs=(pl.BlockSpec((tb, LANES), lambda i: (i, 0)),) * 2,
        name="rope_tables",
    )(positions.reshape(t, 1), inv)


def _rmsnorm_kernel(x_ref, g_ref, o_ref):
    o_ref[...] = _rms(x_ref[...].astype(F32), g_ref[...]).astype(o_ref.dtype)


def _rmsnorm(x, g, out_dtype, tm):
    m, dd = x.shape
    return pl.pallas_call(
        _rmsnorm_kernel,
        out_shape=jax.ShapeDtypeStruct((m, dd), out_dtype),
        grid=(_exact_div(m, tm),),
        in_specs=[pl.BlockSpec((tm, dd), lambda i: (i, 0)), pl.BlockSpec((1, dd), lambda i: (0, 0))],
        out_specs=pl.BlockSpec((tm, dd), lambda i: (i, 0)),
        compiler_params=_cparams(("parallel",), 32),
        name="rmsnorm",
    )(x, g.reshape(1, dd).astype(F32))


def _mm_kernel(*refs, pre, post, n_extra):
    a_ref, w_ref = refs[0], refs[1]
    extras = refs[2:2 + n_extra]
    out_ref = refs[2 + n_extra]
    a = a_ref[...]
    if pre is not None:
        a = pre(a, extras)
    post(_dot(a, w_ref[...]), extras, out_ref)


def _matmul(a, w, *, k, a_col, tm, tn, out_dtype, post, pre=None, extras=(), vmem_mib=48, name):
    m = a.shape[0]
    n = w.shape[1]
    assert w.shape[0] == k
    in_specs = [pl.BlockSpec((tm, k), lambda i, j: (i, a_col)), pl.BlockSpec((k, tn), lambda i, j: (0, j))]
    in_specs += [s for _, s in extras]
    return pl.pallas_call(
        functools.partial(_mm_kernel, pre=pre, post=post, n_extra=len(extras)),
        out_shape=jax.ShapeDtypeStruct((m, n), out_dtype),
        grid=(_exact_div(m, tm), _exact_div(n, tn)),
        in_specs=in_specs,
        out_specs=pl.BlockSpec((tm, tn), lambda i, j: (i, j)),
        compiler_params=_cparams(("parallel", "arbitrary"), vmem_mib),
        name=name,
    )(a, w, *[x for x, _ in extras])


def _post_cast(acc, extras, o_ref):
    o_ref[...] = acc.astype(o_ref.dtype)


def _pre_rms(a, extras):
    return _rms(a.astype(F32), extras[0][...]).astype(BF16)


def _post_residual(acc, extras, o_ref):
    o_ref[...] = extras[0][...] + acc


def _post_in_proj(acc, extras, o_ref, *, j_rope0, j_k, j_rope1, tn, q_scale):
    cos_ref, sin_ref = extras
    j = pl.program_id(1)
    is_rope = (j >= j_rope0) & (j < j_rope1)

    @pl.when(is_rope)
    def _():
        scale = jnp.where(j < j_k, q_scale, 1.0).astype(F32)
        cos = cos_ref[...]
        sin = sin_ref[...]
        for c in range(tn // LANES):
            sl = slice(c * LANES, (c + 1) * LANES)
            o_ref[:, sl] = (_rope128(acc[:, sl], cos, sin) * scale).astype(o_ref.dtype)

    @pl.when(jnp.logical_not(is_rope))
    def _():
        o_ref[...] = acc.astype(o_ref.dtype)


def _post_rope_all(acc, extras, o_ref):
    cos_ref, sin_ref = extras
    cos = cos_ref[...]
    sin = sin_ref[...]
    for c in range(acc.shape[1] // LANES):
        sl = slice(c * LANES, (c + 1) * LANES)
        o_ref[:, sl] = _rope128(acc[:, sl], cos, sin).astype(o_ref.dtype)


def _post_mla_q(acc, extras, o_ref, *, scale):
    _, cos_ref, sin_ref = extras
    cos = cos_ref[...]
    sin = sin_ref[...]
    for hd in range(acc.shape[1] // MLA_HEAD_PAD):
        lo = slice(hd * MLA_HEAD_PAD, hd * MLA_HEAD_PAD + LANES)
        hi = slice(hd * MLA_HEAD_PAD + LANES, (hd + 1) * MLA_HEAD_PAD)
        o_ref[:, lo] = (acc[:, lo] * scale).astype(o_ref.dtype)
        o_ref[:, hi] = (_rope128(acc[:, hi], cos, sin) * scale).astype(o_ref.dtype)


def _merge_kernel(oa_ref, wpa_ref, ob_ref, wpb_ref, ga_ref, gb_ref, o_ref):
    pa = _dot(oa_ref[...], wpa_ref[...])
    pb = _dot(ob_ref[...], wpb_ref[...])
    sa = 1.0 / (1.0 + jnp.exp(-ga_ref[...].astype(F32)))
    sb = 1.0 / (1.0 + jnp.exp(-gb_ref[...].astype(F32)))
    o_ref[...] = (sa * pa + sb * pb).astype(o_ref.dtype)


def _merge(o_a, w_pa, o_b, w_pb, z, d):
    t = d.tokens
    tm, tn = d.tm, d.tn
    ka, kb = o_a.shape[1], o_b.shape[1]
    ja, jb = _exact_div(d.off_ga, tn), _exact_div(d.off_gb, tn)
    return pl.pallas_call(
        _merge_kernel,
        out_shape=jax.ShapeDtypeStruct((t, d.d_model), BF16),
        grid=(_exact_div(t, tm), _exact_div(d.d_model, tn)),
        in_specs=[
            pl.BlockSpec((tm, ka), lambda i, j: (i, 0)),
            pl.BlockSpec((ka, tn), lambda i, j: (0, j)),
            pl.BlockSpec((tm, kb), lambda i, j: (i, 0)),
            pl.BlockSpec((kb, tn), lambda i, j: (0, j)),
            pl.BlockSpec((tm, tn), lambda i, j: (i, j + ja)),
            pl.BlockSpec((tm, tn), lambda i, j: (i, j + jb)),
        ],
        out_specs=pl.BlockSpec((tm, tn), lambda i, j: (i, j)),
        compiler_params=_cparams(("parallel", "arbitrary"), 48),
        name="gated_merge",
    )(o_a, w_pa, o_b, w_pb, z, z)


def _mla_kernel(q_ref, kv_ref, kpe_ref, o_ref, k_scr, *, seq, tq):
    k_scr[:, :NOPE_DIM] = kv_ref[:, :NOPE_DIM]
    k_scr[:, NOPE_DIM:] = kpe_ref[...]
    for i in range(seq // tq):
        ln = (i + 1) * tq
        q = q_ref[i * tq:(i + 1) * tq, :]
        s = _dot_nt(q, k_scr[:ln, :])
        row = lax.broadcasted_iota(jnp.int32, (tq, ln), 0) + i * tq
        col = lax.broadcasted_iota(jnp.int32, (tq, ln), 1)
        s = jnp.where(col <= row, s, NEG_INF)
        m = jnp.max(s, axis=-1, keepdims=True)
        p = jnp.exp(s - m)
        l = jnp.sum(p, axis=-1, keepdims=True)
        o = _dot(p.astype(BF16), kv_ref[:ln, NOPE_DIM:])
        o_ref[i * tq:(i + 1) * tq, :] = (o / l).astype(o_ref.dtype)


def _mla_attention(q_full, kv, kpe, d):
    t = d.tokens
    return pl.pallas_call(
        functools.partial(_mla_kernel, seq=d.seq, tq=d.tq),
        out_shape=jax.ShapeDtypeStruct((t, d.mla_heads * MLA_V_DIM), BF16),
        grid=(d.batch, d.mla_heads),
        in_specs=[
            pl.BlockSpec((d.seq, MLA_HEAD_PAD), lambda b, h: (b, h)),
            pl.BlockSpec((d.seq, NOPE_DIM + MLA_V_DIM), lambda b, h: (b, h)),
            pl.BlockSpec((d.seq, LANES), lambda b, h: (b, 0)),
        ],
        out_specs=pl.BlockSpec((d.seq, MLA_V_DIM), lambda b, h: (b, h)),
        scratch_shapes=[pltpu.VMEM((d.seq, MLA_HEAD_PAD), BF16)],
        compiler_params=_cparams(("parallel", "parallel"), 48),
        name="mla_attention",
    )(q_full, kv, kpe)


def _swa_kernel(sink_ref, q_ref, k_ref, v_ref, o_ref, klo, khi, vlo, vhi, *, seq, window, heads_per_step):
    pair = pl.program_id(1)
    w = window
    lane = lax.broadcasted_iota(jnp.int32, (seq, LANES), 1)
    low = lane < SWA_HD
    for src_ref, lo_ref, hi_ref in ((k_ref, klo, khi), (v_ref, vlo, vhi)):
        x = src_ref[...].astype(F32)
        xs = pltpu.roll(x, SWA_HD, 1)
        zero = jnp.zeros_like(x)
        lo_ref[0] = jnp.where(low, x, zero).astype(BF16)
        hi_ref[0] = jnp.where(low, zero, xs).astype(BF16)
        lo_ref[1] = jnp.where(low, xs, zero).astype(BF16)
        hi_ref[1] = jnp.where(low, zero, x).astype(BF16)

    n_tiles = heads_per_step * SWA_HD // LANES
    tiles_per_kv = n_tiles // 2

    m_rows = tiles_per_kv * w
    row = lax.broadcasted_iota(jnp.int32, (m_rows, 1), 0)
    out_lane = lax.broadcasted_iota(jnp.int32, (m_rows, LANES), 1)

    def mask_bias(klen, is_first):
        qq = lax.broadcasted_iota(jnp.int32, (m_rows, klen), 0) % w
        kk = lax.broadcasted_iota(jnp.int32, (m_rows, klen), 1)
        valid = (kk <= qq) if is_first else ((kk > qq) & (kk <= qq + w))
        return jnp.where(valid, 0.0, NEG_INF).astype(F32)

    def sink_column(g, half):
        sink = jnp.zeros((m_rows, 1), F32)
        for ti in range(tiles_per_kv):
            head = pair * heads_per_step + 2 * (g * tiles_per_kv + ti) + half
            sink = jnp.where((row >= ti * w) & (row < (ti + 1) * w), sink_ref[head], sink)
        return sink

    sinks = [[sink_column(g, half) for half in range(2)] for g in range(2)]

    def block(r0, k0, klen, bias):
        for g in range(2):
            tiles = range(g * tiles_per_kv, (g + 1) * tiles_per_kv)
            q = jnp.concatenate([q_ref[pl.ds(r0, w), c * LANES:(c + 1) * LANES] for c in tiles], axis=0)
            kcat = jnp.concatenate([klo[g, pl.ds(k0, klen), :], khi[g, pl.ds(k0, klen), :]], axis=0)
            vcat = jnp.concatenate([vlo[g, pl.ds(k0, klen), :], vhi[g, pl.ds(k0, klen), :]], axis=0)
            s = _dot_nt(q, kcat)
            probs, inv_den = [], []
            for half in range(2):
                sink = sinks[g][half]
                sh = s[:, half * klen:(half + 1) * klen] + bias
                m = jnp.maximum(jnp.max(sh, axis=-1, keepdims=True), sink)
                p = jnp.exp(sh - m)
                inv_den.append(1.0 / (jnp.sum(p, axis=-1, keepdims=True) + jnp.exp(sink - m)))
                probs.append(p.astype(BF16))
            o = _dot(jnp.concatenate(probs, axis=1), vcat) * jnp.where(out_lane < SWA_HD, inv_den[0], inv_den[1])
            for ti, c in enumerate(tiles):
                o_ref[pl.ds(r0, w), c * LANES:(c + 1) * LANES] = o[ti * w:(ti + 1) * w].astype(o_ref.dtype)

    block(0, 0, w, mask_bias(w, True))
    band = mask_bias(2 * w, False)

    def body(n, carry):
        r0 = pl.multiple_of(n * w, w)
        block(r0, pl.multiple_of(r0 - w, w), 2 * w, band)
        return carry

    n_blk = seq // w
    lax.fori_loop(1, n_blk, body, 0, unroll=3 if (n_blk - 1) % 3 == 0 else 1)


def _swa_attention(z, sinks, d):
    t = d.tokens
    hps = 2 * (d.swa_heads // d.swa_kv_heads)
    qw = hps * SWA_HD
    n_pairs = _exact_div(d.swa_kv_heads, 2)
    jq, jk, jv = _exact_div(d.off_qs, qw), _exact_div(d.off_ks, LANES), _exact_div(d.off_vs, LANES)
    grid_spec = pltpu.PrefetchScalarGridSpec(
        num_scalar_prefetch=1,
        grid=(d.batch, n_pairs),
        in_specs=[
            pl.BlockSpec((d.seq, qw), lambda b, p, s: (b, jq + p)),
            pl.BlockSpec((d.seq, LANES), lambda b, p, s: (b, jk + p)),
            pl.BlockSpec((d.seq, LANES), lambda b, p, s: (b, jv + p)),
        ],
        out_specs=pl.BlockSpec((d.seq, qw), lambda b, p, s: (b, p)),
        scratch_shapes=[pltpu.VMEM((2, d.seq, LANES), BF16)] * 4,
    )
    return pl.pallas_call(
        functools.partial(_swa_kernel, seq=d.seq, window=d.window, heads_per_step=hps),
        out_shape=jax.ShapeDtypeStruct((t, d.swa_q), BF16),
        grid_spec=grid_spec,
        compiler_params=_cparams(("parallel", "parallel"), 48),
        name="swa_attention",
    )(sinks.astype(F32), z, z, z)


def _cross_router_kernel(h_ref, gc_ref, wq_ref, kvm_ref, wo_ref, gf_ref, wr_ref, br_ref,
                         h2_ref, xnf_ref, rt_ref, cnt_ref, run_ref, *, x_heads, n_groups, scale):
    tm = h_ref.shape[0]
    hx = x_heads * X_HD
    n_exp = n_groups * EPG

    @pl.when(pl.program_id(0) == 0)
    def _():
        run_ref[...] = jnp.zeros_like(run_ref)

    h = h_ref[...]
    hn = _rms(h, gc_ref[...]).astype(BF16)
    q = (_dot(hn, wq_ref[...]) * scale).astype(BF16)
    outs = []
    for hd in range(x_heads):
        kh = kvm_ref[:, hd * X_HD:(hd + 1) * X_HD]
        vh = kvm_ref[:, hx + hd * X_HD:hx + (hd + 1) * X_HD]
        s = _dot_nt(q[:, hd * X_HD:(hd + 1) * X_HD], kh)
        m = jnp.max(s, axis=-1, keepdims=True)
        p = jnp.exp(s - m)
        l = jnp.sum(p, axis=-1, keepdims=True)
        outs.append((_dot(p.astype(BF16), vh) / l).astype(BF16))
    h2 = h + _dot(jnp.concatenate(outs, axis=1), wo_ref[...])
    h2_ref[...] = h2
    xnf = _rms(h2, gf_ref[...])
    xnf_ref[...] = xnf

    logits = _dot(xnf.astype(BF16), wr_ref[...]) + br_ref[...]
    lane = lax.broadcasted_iota(jnp.int32, (tm, LANES), 1).astype(F32)
    big = float(LANES)
    is_group = (lane >= n_exp) & (lane < n_exp + n_groups)
    gl = jnp.where(is_group, logits, -jnp.inf)
    gmax = jnp.max(gl, axis=-1, keepdims=True)
    g_lane = jnp.min(jnp.where(gl == gmax, lane, big), axis=-1, keepdims=True)
    p_group = 1.0 / jnp.sum(jnp.where(is_group, jnp.exp(gl - gmax), 0.0), axis=-1, keepdims=True)
    e_lo = (g_lane - n_exp) * EPG
    in_group = (lane >= e_lo) & (lane < e_lo + EPG)
    el = jnp.where(in_group, logits, -jnp.inf)
    m1 = jnp.max(el, axis=-1, keepdims=True)
    i1 = jnp.min(jnp.where(el == m1, lane, big), axis=-1, keepdims=True)
    el2 = jnp.where(lane == i1, -jnp.inf, el)
    m2 = jnp.max(el2, axis=-1, keepdims=True)
    i2 = jnp.min(jnp.where(el2 == m2, lane, big), axis=-1, keepdims=True)
    w2 = jnp.exp(m2 - m1)
    gate1 = p_group / (1.0 + w2)
    gate2 = gate1 * w2

    hot1 = lane == i1
    hot2 = lane == i2
    onehot = jnp.where(hot1 | hot2, 1.0, 0.0)
    rr = lax.broadcasted_iota(jnp.int32, (tm, tm), 0)
    cc = lax.broadcasted_iota(jnp.int32, (tm, tm), 1)
    tri = jnp.where(cc < rr, 1.0, 0.0).astype(BF16)
    before = _dot(tri, onehot.astype(BF16)) + run_ref[...]
    rank1 = jnp.sum(jnp.where(hot1, before, 0.0), axis=-1, keepdims=True)
    rank2 = jnp.sum(jnp.where(hot2, before, 0.0), axis=-1, keepdims=True)
    run = run_ref[...] + jnp.sum(onehot, axis=0, keepdims=True)
    run_ref[...] = run
    cnt_ref[...] = run

    rt = jnp.where(lane == 0, i1, 0.0)
    rt = jnp.where(lane == 1, i2, rt)
    rt = jnp.where(lane == 2, gate1, rt)
    rt = jnp.where(lane == 3, gate2, rt)
    rt = jnp.where(lane == 4, rank1, rt)
    rt = jnp.where(lane == 5, rank2, rt)
    rt_ref[...] = rt


def _cross_router(h, g_cross, w_xq, kvm, w_xo, g_ffn, w_r, b_r, d):
    t = d.tokens
    tm = d.t_cross
    dm = d.d_model
    hx = d.x_heads * X_HD
    steps_per_batch = _exact_div(d.seq, tm)
    row = lambda i: (i, 0)
    fixed = lambda i: (0, 0)
    return pl.pallas_call(
        functools.partial(_cross_router_kernel, x_heads=d.x_heads, n_groups=d.n_groups, scale=X_HD ** -0.5),
        out_shape=(
            jax.ShapeDtypeStruct((t, dm), F32),
            jax.ShapeDtypeStruct((t, dm), F32),
            jax.ShapeDtypeStruct((t, LANES), F32),
            jax.ShapeDtypeStruct((1, LANES), F32),
        ),
        grid=(_exact_div(t, tm),),
        in_specs=[
            pl.BlockSpec((tm, dm), row),
            pl.BlockSpec((1, dm), fixed),
            pl.BlockSpec((dm, hx), fixed, pipeline_mode=pl.Buffered(1)),
            pl.BlockSpec((d.mem_len, 2 * hx), lambda i: (i // steps_per_batch, 0)),
            pl.BlockSpec((hx, dm), fixed, pipeline_mode=pl.Buffered(1)),
            pl.BlockSpec((1, dm), fixed),
            pl.BlockSpec((dm, LANES), fixed, pipeline_mode=pl.Buffered(1)),
            pl.BlockSpec((1, LANES), fixed),
        ],
        out_specs=(
            pl.BlockSpec((tm, dm), row),
            pl.BlockSpec((tm, dm), row),
            pl.BlockSpec((tm, LANES), row),
            pl.BlockSpec((1, LANES), fixed),
        ),
        scratch_shapes=[pltpu.VMEM((1, LANES), F32)],
        compiler_params=_cparams(("arbitrary",), 56),
        name="cross_attention_router",
    )(h, g_cross.reshape(1, dm), w_xq, kvm, w_xo, g_ffn.reshape(1, dm), w_r, b_r)


def _row_copy(src_hbm, src_row, dst_ref, dst_row, sem):
    return pltpu.make_async_copy(src_hbm.at[pl.ds(src_row, 1)], dst_ref.at[pl.ds(dst_row, 1)], sem)


def _dispatch_kernel(dest_ref, fill_ref, x_ref, xs_hbm, zero_ref, sem, zrow_sem, zblk_sem, *,
                     t_tok, n_experts, moe_block, n_blocks):
    base = pl.program_id(0) * t_tok

    @pl.when(pl.program_id(0) == 0)
    def _():
        zero_ref[...] = jnp.zeros_like(zero_ref)
        zero_block = lambda b: pltpu.make_async_copy(
            zero_ref, xs_hbm.at[pl.ds(pl.multiple_of(b * moe_block, moe_block), moe_block)], zblk_sem)

        def for_fill_rows(fn):
            for e in range(n_experts):
                lax.fori_loop(fill_ref[2 * e], fill_ref[2 * e + 1], fn, 0)

        def start_row(r, carry):
            _row_copy(zero_ref, 0, xs_hbm, r, zrow_sem).start()
            return carry

        def wait_row(r, carry):
            _row_copy(zero_ref, 0, xs_hbm, 0, zrow_sem).wait()
            return carry

        def start_block(b, carry):
            zero_block(b).start()
            return carry

        def wait_block(b, carry):
            zero_block(0).wait()
            return carry

        for_fill_rows(start_row)
        lax.fori_loop(fill_ref[2 * n_experts], n_blocks, start_block, 0)
        for_fill_rows(wait_row)
        lax.fori_loop(fill_ref[2 * n_experts], n_blocks, wait_block, 0)

    def issue(r, carry):
        for k in range(TOP_K):
            _row_copy(x_ref, r, xs_hbm, dest_ref[TOP_K * (base + r) + k], sem).start()
        return carry

    lax.fori_loop(0, t_tok, issue, 0, unroll=8)
    for k in range(TOP_K):
        pltpu.make_async_copy(x_ref, x_ref, sem).wait()


def _dispatch(dest, fill, x, d):
    t = d.tokens
    p_rows = d.n_blocks * d.moe_block
    grid_spec = pltpu.PrefetchScalarGridSpec(
        num_scalar_prefetch=2,
        grid=(_exact_div(t, d.t_tok),),
        in_specs=[pl.BlockSpec((d.t_tok, d.d_model), lambda i, dest, fill: (i, 0))],
        out_specs=pl.BlockSpec(memory_space=pl.ANY),
        scratch_shapes=[pltpu.VMEM((d.moe_block, d.d_model), x.dtype)] + [pltpu.SemaphoreType.DMA(())] * 3,
    )
    return pl.pallas_call(
        functools.partial(_dispatch_kernel, t_tok=d.t_tok, n_experts=d.n_experts, moe_block=d.moe_block,
                          n_blocks=d.n_blocks),
        out_shape=jax.ShapeDtypeStruct((p_rows, d.d_model), x.dtype),
        grid_spec=grid_spec,
        compiler_params=_cparams(("arbitrary",), 32),
        name="moe_dispatch",
    )(dest, fill, x)


W_SLABS = 4


def _moe_steps(d):
    return W_SLABS + d.n_blocks + W_SLABS * d.n_experts + 1


def _moe_schedule(padded, d):
    bm, ne, nblk = d.moe_block, d.n_experts, d.n_blocks
    i32 = jnp.int32
    nb = padded // bm
    first_blk = (jnp.cumsum(padded) - padded) // bm
    n_used = jnp.sum(nb)
    per_expert = jnp.where(jnp.arange(ne) == ne - 1, nb, jnp.maximum(nb, W_SLABS))
    step_end = W_SLABS + jnp.cumsum(per_expert)
    step_start = step_end - per_expert
    total = step_end[-1]
    s = jnp.arange(_moe_steps(d), dtype=i32)
    is_pro = s < W_SLABS
    is_tail = s >= total
    e = jnp.minimum(jnp.sum((step_end[None, :] <= s[:, None]).astype(i32), axis=1), ne - 1)
    j = jnp.where(is_pro, s, s - step_start[e])
    has_blk = jnp.logical_not(is_pro | is_tail) & (j < nb[e])
    blk = jnp.where(is_pro, 0, first_blk[e] + jnp.minimum(j, nb[e]))
    xblk = jnp.where(has_blk, blk, 0)
    oblk = jnp.where(is_tail, jnp.minimum(n_used + s - total, nblk), blk)
    nxt = jnp.where(is_pro, 0, jnp.minimum(e + 1, ne - 1))
    slab = jnp.where(is_tail, W_SLABS - 1, jnp.minimum(j, W_SLABS - 1))
    slot = jnp.where(is_pro, 1, e % 2)
    return tuple(v.astype(i32) for v in (xblk, oblk, nxt, slab, slot, is_tail))


def _silu_mul(a, b):
    return (a / (1.0 + jnp.exp(-a))) * b


def _expert_up_kernel(xblk, oblk, nxt, slab, slot, tail, x_ref, wg_st, wu_st, hb_ref, wg0, wg1, wu0, wu1):
    del xblk, oblk, nxt
    s = pl.program_id(0)
    rows = wg_st.shape[1]

    @pl.when(s == 0)
    def _():
        wg1[...] = jnp.zeros_like(wg1)
        wu1[...] = jnp.zeros_like(wu1)

    @pl.when(tail[s] == 1)
    def _():
        hb_ref[...] = jnp.zeros_like(hb_ref)

    r0 = pl.multiple_of(slab[s] * rows, rows)
    for cur, (wg_c, wu_c, wg_n, wu_n) in enumerate(((wg0, wu0, wg1, wu1), (wg1, wu1, wg0, wu0))):
        @pl.when((tail[s] == 0) & (slot[s] == cur))
        def _():
            wg_n[pl.ds(r0, rows), :] = wg_st[0].astype(BF16)
            wu_n[pl.ds(r0, rows), :] = wu_st[0].astype(BF16)
            x = x_ref[...].astype(BF16)
            hb_ref[...] = _silu_mul(_dot(x, wg_c[...]), _dot(x, wu_c[...])).astype(hb_ref.dtype)


def _expert_down_kernel(xblk, oblk, nxt, slab, slot, tail, hb_ref, wd_st, y_ref, wd0, wd1):
    del xblk, oblk, nxt
    s = pl.program_id(0)
    rows = wd_st.shape[1]

    @pl.when(s == 0)
    def _():
        wd1[...] = jnp.zeros_like(wd1)

    @pl.when(tail[s] == 1)
    def _():
        y_ref[...] = jnp.zeros_like(y_ref)

    r0 = pl.multiple_of(slab[s] * rows, rows)
    for cur, (wd_c, wd_n) in enumerate(((wd0, wd1), (wd1, wd0))):
        @pl.when((tail[s] == 0) & (slot[s] == cur))
        def _():
            wd_n[pl.ds(r0, rows), :] = wd_st[0].astype(BF16)
            y_ref[...] = _dot(hb_ref[...], wd_c[...])


def _experts(sched, xs, w_gate, w_up, w_down, layer, d):
    bm = d.moe_block
    dm, de = d.d_model, d.d_expert
    ne = d.n_experts
    rows_out = (d.n_blocks + 1) * bm
    w_gate, w_up = w_gate.reshape(-1, dm, de), w_up.reshape(-1, dm, de)
    w_down = w_down.reshape(-1, de, dm)
    x_map = lambda s, xb, ob, nx, sl, st, tl: (xb[s], 0)
    o_map = lambda s, xb, ob, nx, sl, st, tl: (ob[s], 0)
    w_map = lambda s, xb, ob, nx, sl, st, tl: (layer * ne + nx[s], sl[s], 0)
    ku, kd = _exact_div(dm, W_SLABS), _exact_div(de, W_SLABS)
    hb = pl.pallas_call(
        _expert_up_kernel,
        out_shape=jax.ShapeDtypeStruct((rows_out, de), BF16),
        grid_spec=pltpu.PrefetchScalarGridSpec(
            num_scalar_prefetch=len(sched),
            grid=(_moe_steps(d),),
            in_specs=[pl.BlockSpec((bm, dm), x_map), pl.BlockSpec((1, ku, de), w_map),
                      pl.BlockSpec((1, ku, de), w_map)],
            out_specs=pl.BlockSpec((bm, de), o_map),
            scratch_shapes=[pltpu.VMEM((dm, de), BF16)] * 4,
        ),
        compiler_params=_cparams(("arbitrary",), 56),
        name="moe_experts_up",
    )(*sched, xs, w_gate, w_up)
    return pl.pallas_call(
        _expert_down_kernel,
        out_shape=jax.ShapeDtypeStruct((rows_out, dm), F32),
        grid_spec=pltpu.PrefetchScalarGridSpec(
            num_scalar_prefetch=len(sched),
            grid=(_moe_steps(d),),
            in_specs=[pl.BlockSpec((bm, de), x_map), pl.BlockSpec((1, kd, dm), w_map)],
            out_specs=pl.BlockSpec((bm, dm), o_map),
            scratch_shapes=[pltpu.VMEM((de, dm), BF16)] * 2,
        ),
        compiler_params=_cparams(("arbitrary",), 40),
        name="moe_experts_down",
    )(*sched, hb, w_down)


def _combine_kernel(dest_ref, y_hbm, h_ref, rt_ref, g_ref, *rest, t_tok, last):
    if last:
        out_ref, ybuf, sem = rest
    else:
        h3_ref, out_ref, ybuf, sem = rest
    i = pl.program_id(0)
    buf = i % 2

    def gather(step, b):
        base = step * t_tok

        def issue(r, carry):
            for k in range(TOP_K):
                _row_copy(y_hbm, dest_ref[TOP_K * (base + r) + k], ybuf.at[b, k], r, sem.at[b]).start()
            return carry

        lax.fori_loop(0, t_tok, issue, 0, unroll=8)

    @pl.when(i == 0)
    def _():
        gather(0, 0)

    @pl.when(i + 1 < pl.num_programs(0))
    def _():
        gather(i + 1, 1 - buf)

    pltpu.make_async_copy(ybuf.at[buf], ybuf.at[buf], sem.at[buf]).wait()
    rt = rt_ref[...]
    h3 = h_ref[...] + rt[:, 2:3] * ybuf[buf, 0] + rt[:, 3:4] * ybuf[buf, 1]
    if not last:
        h3_ref[...] = h3
    out_ref[...] = _rms(h3, g_ref[...]).astype(out_ref.dtype)


def _combine(dest, y, h2, rt, g_next, d, last):
    t = d.tokens
    tt = d.t_tok
    dm = d.d_model
    row = lambda i, ds: (i, 0)
    out_specs = pl.BlockSpec((tt, dm), row)
    if last:
        out_shape = jax.ShapeDtypeStruct((t, dm), F32)
    else:
        out_shape = (jax.ShapeDtypeStruct((t, dm), F32), jax.ShapeDtypeStruct((t, dm), BF16))
        out_specs = (out_specs, pl.BlockSpec((tt, dm), row))
    grid_spec = pltpu.PrefetchScalarGridSpec(
        num_scalar_prefetch=1,
        grid=(_exact_div(t, tt),),
        in_specs=[
            pl.BlockSpec(memory_space=pl.ANY),
            pl.BlockSpec((tt, dm), row),
            pl.BlockSpec((tt, LANES), row),
            pl.BlockSpec((1, dm), lambda i, ds: (0, 0)),
        ],
        out_specs=out_specs,
        scratch_shapes=[pltpu.VMEM((2, TOP_K, tt, dm), F32), pltpu.SemaphoreType.DMA((2,))],
    )
    return pl.pallas_call(
        functools.partial(_combine_kernel, t_tok=tt, last=last),
        out_shape=out_shape,
        grid_spec=grid_spec,
        compiler_params=_cparams(("arbitrary",), 48),
        name="moe_combine",
    )(dest, y, h2, rt, g_next.reshape(1, dm))


def _layer(h, xn, cos, sin, memn, p, d, g_next, last):
    t = d.tokens
    dm = d.d_model
    tm = d.tm
    row128 = pl.BlockSpec((tm, LANES), lambda i, j: (i, 0))

    tn = d.tn_in
    row128_in = pl.BlockSpec((d.tm_in, LANES), lambda i, j: (i, 0))
    z = _matmul(
        xn, p["w_main"], k=dm, a_col=0, tm=d.tm_in, tn=tn, out_dtype=BF16, vmem_mib=60,
        post=functools.partial(_post_in_proj, j_rope0=_exact_div(d.off_qs, tn), j_k=_exact_div(d.off_ks, tn),
                               j_rope1=_exact_div(d.off_vs, tn), tn=tn, q_scale=SWA_HD ** -0.5),
        extras=((cos, row128_in), (sin, row128_in)), name="in_proj")
    kpe = _matmul(xn, p["w_kr"], k=dm, a_col=0, tm=tm, tn=LANES, out_dtype=BF16, post=_post_rope_all,
                  extras=((cos, row128), (sin, row128)), name="rope_key_proj")

    tn_q = min(4, d.mla_heads) * MLA_HEAD_PAD
    q_full = _matmul(
        z, p["w_uq"], k=d.q_lora, a_col=0, tm=tm, tn=tn_q, out_dtype=BF16, pre=_pre_rms,
        post=functools.partial(_post_mla_q, scale=(NOPE_DIM + ROPE_DIM) ** -0.5),
        extras=((p["g_qa"], pl.BlockSpec((1, d.q_lora), lambda i, j: (0, 0))), (cos, row128), (sin, row128)),
        name="mla_q_proj")
    kv = _matmul(
        z, p["w_ukv"], k=d.kv_lora, a_col=_exact_div(d.off_ckv, d.kv_lora), tm=tm, tn=tn_q, out_dtype=BF16,
        pre=_pre_rms, post=_post_cast,
        extras=((p["g_kva"], pl.BlockSpec((1, d.kv_lora), lambda i, j: (0, 0))),), name="mla_kv_proj")
    o_a = _mla_attention(q_full, kv, kpe, d)

    o_b = _swa_attention(z, p["sinks"], d)

    merged = _merge(o_a, p["w_pa"], o_b, p["w_pb"], z, d)
    h1 = _matmul(merged, p["w_o"], k=dm, a_col=0, tm=tm, tn=d.tn, out_dtype=F32, post=_post_residual,
                 extras=((h, pl.BlockSpec((tm, d.tn), lambda i, j: (i, j))),), name="out_proj")

    kvm = _matmul(memn, p["w_xkv"], k=dm, a_col=0, tm=memn.shape[0], tn=d.tn, out_dtype=BF16,
                  post=_post_cast, name="mem_kv_proj")
    h2, xnf, rt, cnt = _cross_router(h1, p["g_cross"], p["w_xq"], kvm, p["w_xo"], p["g_ffn"],
                                     p["w_r"], p["b_r"], d)

    bm = d.moe_block
    counts = cnt[0, :d.n_experts].astype(jnp.int32)
    padded = (counts + bm - 1) // bm * bm
    pad_ends = jnp.cumsum(padded)
    pad_starts = pad_ends - padded
    expert = rt[:, 0:TOP_K].astype(jnp.int32)
    dest = (pad_starts[expert] + rt[:, 4:4 + TOP_K].astype(jnp.int32)).reshape(t * TOP_K)
    fill = jnp.concatenate([jnp.stack([pad_starts + counts, pad_ends], axis=1).reshape(-1),
                            pad_ends[-1:] // bm]).astype(jnp.int32)

    xs = _dispatch(dest, fill, xnf, d)
    y = _experts(_moe_schedule(padded, d), xs, p["w_gate"], p["w_up"], p["w_down"], p["layer"], d)
    return _combine(dest, y, h2, rt, g_next, d, last)


def _prep_layer(l, d, g_mix, w_in, g_qa, g_kva, w_uq, w_ukv, sinks, w_pa, w_pb, w_o, g_cross, w_xq, w_xkv,
                w_xo, g_ffn, w_group, b_group, w_router, b_router, w_gate, w_up, w_down):
    dm = d.d_model
    kr0 = d.q_lora + d.kv_lora
    w = w_in[l]
    w_main = jnp.concatenate([w[:, :kr0].astype(BF16), w[:, kr0 + ROPE_DIM:].astype(BF16)], axis=1)
    w_kr = jnp.pad(w[:, kr0:kr0 + ROPE_DIM], ((0, 0), (0, LANES - ROPE_DIM))).astype(BF16)
    qk = NOPE_DIM + ROPE_DIM
    wq = w_uq[l].reshape(d.q_lora, d.mla_heads, qk)
    wq = jnp.pad(wq, ((0, 0), (0, 0), (0, MLA_HEAD_PAD - qk))).reshape(d.q_lora, d.mla_heads * MLA_HEAD_PAD)
    n_r = d.n_experts + d.n_groups
    w_r = jnp.pad(jnp.concatenate([w_router[l], w_group[l]], axis=1), ((0, 0), (0, LANES - n_r)))
    b_r = jnp.pad(jnp.concatenate([b_router[l], b_group[l]]), (0, LANES - n_r)).reshape(1, LANES)
    return dict(
        w_main=w_main, w_kr=w_kr, w_uq=wq.astype(BF16), w_ukv=w_ukv[l].astype(BF16),
        g_qa=g_qa[l].reshape(1, -1).astype(F32), g_kva=g_kva[l].reshape(1, -1).astype(F32),
        sinks=sinks[l], w_pa=w_pa[l].astype(BF16), w_pb=w_pb[l].astype(BF16), w_o=w_o[l].astype(BF16),
        g_cross=g_cross[l].astype(F32), w_xq=w_xq[l].astype(BF16), w_xkv=w_xkv[l].astype(BF16),
        w_xo=w_xo[l].astype(BF16), g_ffn=g_ffn[l].astype(F32), w_r=w_r.astype(BF16), b_r=b_r.astype(F32),
        w_gate=w_gate, w_up=w_up, w_down=w_down, layer=l,
    )


def _forward(d, x, mem, positions, g_mix, w_in, g_qa, g_kva, w_uq, w_ukv, sinks, w_pa, w_pb, w_o,
             g_cross, g_mem, w_xq, w_xkv, w_xo, g_ffn, w_group, b_group, w_router, b_router,
             w_gate, w_up, w_down, g_final):
    depth = w_in.shape[0]
    t = d.tokens
    dm = d.d_model
    cos, sin = _rope_tables(positions, d)
    memn = _rmsnorm(mem.reshape(d.batch * d.mem_len, dm), g_mem, BF16, d.t_norm)
    h = x.reshape(t, dm)
    xn = _rmsnorm(h, g_mix[0], BF16, d.t_norm)
    for l in range(depth):
        p = _prep_layer(l, d, g_mix, w_in, g_qa, g_kva, w_uq, w_ukv, sinks, w_pa, w_pb, w_o, g_cross,
                        w_xq, w_xkv, w_xo, g_ffn, w_group, b_group, w_router, b_router, w_gate, w_up, w_down)
        last = l == depth - 1
        g_next = g_final if last else g_mix[l + 1]
        res = _layer(h, xn, cos, sin, memn, p, d, g_next, last)
        if last:
            return res.reshape(d.batch, d.seq, dm)
        h, xn = res


def kernel(x, mem, positions, g_mix, w_in, g_qa, g_kva, w_uq, w_ukv, sinks, w_pa, w_pb, w_o, g_cross, g_mem,
           w_xq, w_xkv, w_xo, g_ffn, w_group, b_group, w_router, b_router, w_gate, w_up, w_down, g_final):
    return _forward(Dims(), x, mem, positions, g_mix, w_in, g_qa, g_kva, w_uq, w_ukv, sinks, w_pa, w_pb, w_o,
                    g_cross, g_mem, w_xq, w_xkv, w_xo, g_ffn, w_group, b_group, w_router, b_router,
                    w_gate, w_up, w_down, g_final)
```

```python
import functools
from typing import NamedTuple

import jax
import jax.numpy as jnp
from jax import lax
from jax.experimental import pallas as pl
from jax.experimental.pallas import tpu as pltpu

F32 = jnp.float32
BF16 = jnp.bfloat16
EPS = 1e-6
ROPE_THETA = 10000.0
NEG_INF = -1e30
LANES = 128
ROPE_DIM = 64
NOPE_DIM = 128
MLA_V_DIM = 128
MLA_HEAD_PAD = 256
SWA_HD = 64
X_HD = 128
EPG = 8
TOP_K = 2
MIB = 1024 * 1024


class Dims(NamedTuple):
    batch: int = 4
    seq: int = 2048
    d_model: int = 4096
    mem_len: int = 256
    mla_heads: int = 16
    q_lora: int = 1024
    kv_lora: int = 512
    swa_heads: int = 32
    swa_kv_heads: int = 8
    window: int = 128
    x_heads: int = 4
    n_groups: int = 4
    d_expert: int = 768
    moe_block: int = 128
    tm: int = 1024
    tm_in: int = 2048
    tn_in: int = 512
    tn: int = 512
    tq: int = 512
    t_cross: int = 256
    t_tok: int = 256
    t_norm: int = 256

    @property
    def tokens(self):
        return self.batch * self.seq

    @property
    def n_experts(self):
        return self.n_groups * EPG

    @property
    def swa_q(self):
        return self.swa_heads * SWA_HD

    @property
    def swa_kv(self):
        return self.swa_kv_heads * SWA_HD

    @property
    def off_ckv(self):
        return self.q_lora

    @property
    def off_qs(self):
        return self.q_lora + self.kv_lora

    @property
    def off_ks(self):
        return self.off_qs + self.swa_q

    @property
    def off_vs(self):
        return self.off_ks + self.swa_kv

    @property
    def off_ga(self):
        return self.off_vs + self.swa_kv

    @property
    def off_gb(self):
        return self.off_ga + self.d_model

    @property
    def n_main(self):
        return self.off_gb + self.d_model

    @property
    def n_blocks(self):
        return -(-(self.tokens * TOP_K) // self.moe_block) + self.n_experts


def _exact_div(a, b):
    assert a % b == 0, (a, b)
    return a // b


def _cparams(sem, vmem_mib):
    return pltpu.CompilerParams(dimension_semantics=sem, vmem_limit_bytes=vmem_mib * MIB)


def _rms(x, g):
    return x * lax.rsqrt(jnp.mean(x * x, axis=-1, keepdims=True) + EPS) * g


def _rope128(x, cos, sin):
    lane = lax.broadcasted_iota(jnp.int32, x.shape, 1)
    first_half = (lane % ROPE_DIM) < (ROPE_DIM // 2)
    rot = jnp.where(first_half, -pltpu.roll(x, LANES - ROPE_DIM // 2, 1), pltpu.roll(x, ROPE_DIM // 2, 1))
    return x * cos + rot * sin


def _dot(a, b):
    return jnp.dot(a, b, preferred_element_type=F32)


def _dot_nt(a, b):
    return lax.dot_general(a, b, (((1,), (1,)), ((), ())), preferred_element_type=F32)


def _rope_table_kernel(pos_ref, inv_ref, cos_ref, sin_ref):
    ang = pos_ref[...].astype(F32) * inv_ref[...]
    cos_ref[...] = jnp.cos(ang)
    sin_ref[...] = jnp.sin(ang)


def _rope_tables(positions, d):
    t = d.tokens
    half = ROPE_DIM // 2
    inv_freq = 1.0 / (ROPE_THETA ** (jnp.arange(0, ROPE_DIM, 2, dtype=F32) / ROPE_DIM))
    inv = jnp.tile(inv_freq, LANES // half).reshape(1, LANES)
    tb = min(t, 1024)
    return pl.pallas_call(
        _rope_table_kernel,
        out_shape=(jax.ShapeDtypeStruct((t, LANES), F32),) * 2,
        grid=(_exact_div(t, tb),),
        in_specs=[pl.BlockSpec((tb, 1), lambda i: (i, 0)), pl.BlockSpec((1, LANES), lambda i: (0, 0))],
        out_specs=(pl.BlockSpec((tb, LANES), lambda i: (i, 0)),) * 2,
        name="rope_tables",
    )(positions.reshape(t, 1), inv)


def _rmsnorm_kernel(x_ref, g_ref, o_ref):
    o_ref[...] = _rms(x_ref[...].astype(F32), g_ref[...]).astype(o_ref.dtype)


def _rmsnorm(x, g, out_dtype, tm):
    m, dd = x.shape
    return pl.pallas_call(
        _rmsnorm_kernel,
        out_shape=jax.ShapeDtypeStruct((m, dd), out_dtype),
        grid=(_exact_div(m, tm),),
        in_specs=[pl.BlockSpec((tm, dd), lambda i: (i, 0)), pl.BlockSpec((1, dd), lambda i: (0, 0))],
        out_specs=pl.BlockSpec((tm, dd), lambda i: (i, 0)),
        compiler_params=_cparams(("parallel",), 32),
        name="rmsnorm",
    )(x, g.reshape(1, dd).astype(F32))


def _mm_kernel(*refs, pre, post, n_extra):
    a_ref, w_ref = refs[0], refs[1]
    extras = refs[2:2 + n_extra]
    out_ref = refs[2 + n_extra]
    a = a_ref[...]
    if pre is not None:
        a = pre(a, extras)
    post(_dot(a, w_ref[...]), extras, out_ref)


def _matmul(a, w, *, k, a_col, tm, tn, out_dtype, post, pre=None, extras=(), vmem_mib=48, name):
    m = a.shape[0]
    n = w.shape[1]
    assert w.shape[0] == k
    in_specs = [pl.BlockSpec((tm, k), lambda i, j: (i, a_col)), pl.BlockSpec((k, tn), lambda i, j: (0, j))]
    in_specs += [s for _, s in extras]
    return pl.pallas_call(
        functools.partial(_mm_kernel, pre=pre, post=post, n_extra=len(extras)),
        out_shape=jax.ShapeDtypeStruct((m, n), out_dtype),
        grid=(_exact_div(m, tm), _exact_div(n, tn)),
        in_specs=in_specs,
        out_specs=pl.BlockSpec((tm, tn), lambda i, j: (i, j)),
        compiler_params=_cparams(("parallel", "arbitrary"), vmem_mib),
        name=name,
    )(a, w, *[x for x, _ in extras])


def _post_cast(acc, extras, o_ref):
    o_ref[...] = acc.astype(o_ref.dtype)


def _pre_rms(a, extras):
    return _rms(a.astype(F32), extras[0][...]).astype(BF16)


def _post_residual(acc, extras, o_ref):
    o_ref[...] = extras[0][...] + acc


def _post_in_proj(acc, extras, o_ref, *, j_rope0, j_k, j_rope1, tn, q_scale):
    cos_ref, sin_ref = extras
    j = pl.program_id(1)
    is_rope = (j >= j_rope0) & (j < j_rope1)

    @pl.when(is_rope)
    def _():
        scale = jnp.where(j < j_k, q_scale, 1.0).astype(F32)
        cos = cos_ref[...]
        sin = sin_ref[...]
        for c in range(tn // LANES):
            sl = slice(c * LANES, (c + 1) * LANES)
            o_ref[:, sl] = (_rope128(acc[:, sl], cos, sin) * scale).astype(o_ref.dtype)

    @pl.when(jnp.logical_not(is_rope))
    def _():
        o_ref[...] = acc.astype(o_ref.dtype)


def _post_rope_all(acc, extras, o_ref):
    cos_ref, sin_ref = extras
    cos = cos_ref[...]
    sin = sin_ref[...]
    for c in range(acc.shape[1] // LANES):
        sl = slice(c * LANES, (c + 1) * LANES)
        o_ref[:, sl] = _rope128(acc[:, sl], cos, sin).astype(o_ref.dtype)


def _post_mla_q(acc, extras, o_ref, *, scale):
    _, cos_ref, sin_ref = extras
    cos = cos_ref[...]
    sin = sin_ref[...]
    for hd in range(acc.shape[1] // MLA_HEAD_PAD):
        lo = slice(hd * MLA_HEAD_PAD, hd * MLA_HEAD_PAD + LANES)
        hi = slice(hd * MLA_HEAD_PAD + LANES, (hd + 1) * MLA_HEAD_PAD)
        o_ref[:, lo] = (acc[:, lo] * scale).astype(o_ref.dtype)
        o_ref[:, hi] = (_rope128(acc[:, hi], cos, sin) * scale).astype(o_ref.dtype)


def _merge_kernel(oa_ref, wpa_ref, ob_ref, wpb_ref, ga_ref, gb_ref, o_ref):
    pa = _dot(oa_ref[...], wpa_ref[...])
    pb = _dot(ob_ref[...], wpb_ref[...])
    sa = 1.0 / (1.0 + jnp.exp(-ga_ref[...].astype(F32)))
    sb = 1.0 / (1.0 + jnp.exp(-gb_ref[...].astype(F32)))
    o_ref[...] = (sa * pa + sb * pb).astype(o_ref.dtype)


def _merge(o_a, w_pa, o_b, w_pb, z, d):
    t = d.tokens
    tm, tn = d.tm, d.tn
    ka, kb = o_a.shape[1], o_b.shape[1]
    ja, jb = _exact_div(d.off_ga, tn), _exact_div(d.off_gb, tn)
    return pl.pallas_call(
        _merge_kernel,
        out_shape=jax.ShapeDtypeStruct((t, d.d_model), BF16),
        grid=(_exact_div(t, tm), _exact_div(d.d_model, tn)),
        in_specs=[
            pl.BlockSpec((tm, ka), lambda i, j: (i, 0)),
            pl.BlockSpec((ka, tn), lambda i, j: (0, j)),
            pl.BlockSpec((tm, kb), lambda i, j: (i, 0)),
            pl.BlockSpec((kb, tn), lambda i, j: (0, j)),
            pl.BlockSpec((tm, tn), lambda i, j: (i, j + ja)),
            pl.BlockSpec((tm, tn), lambda i, j: (i, j + jb)),
        ],
        out_specs=pl.BlockSpec((tm, tn), lambda i, j: (i, j)),
        compiler_params=_cparams(("parallel", "arbitrary"), 48),
        name="gated_merge",
    )(o_a, w_pa, o_b, w_pb, z, z)


def _mla_kernel(q_ref, kv_ref, kpe_ref, o_ref, k_scr, *, seq, tq):
    k_scr[:, :NOPE_DIM] = kv_ref[:, :NOPE_DIM]
    k_scr[:, NOPE_DIM:] = kpe_ref[...]
    for i in range(seq // tq):
        ln = (i + 1) * tq
        q = q_ref[i * tq:(i + 1) * tq, :]
        s = _dot_nt(q, k_scr[:ln, :])
        row = lax.broadcasted_iota(jnp.int32, (tq, ln), 0) + i * tq
        col = lax.broadcasted_iota(jnp.int32, (tq, ln), 1)
        s = jnp.where(col <= row, s, NEG_INF)
        m = jnp.max(s, axis=-1, keepdims=True)
        p = jnp.exp(s - m)
        l = jnp.sum(p, axis=-1, keepdims=True)
        o = _dot(p.astype(BF16), kv_ref[:ln, NOPE_DIM:])
        o_ref[i * tq:(i + 1) * tq, :] = (o / l).astype(o_ref.dtype)


def _mla_attention(q_full, kv, kpe, d):
    t = d.tokens
    return pl.pallas_call(
        functools.partial(_mla_kernel, seq=d.seq, tq=d.tq),
        out_shape=jax.ShapeDtypeStruct((t, d.mla_heads * MLA_V_DIM), BF16),
        grid=(d.batch, d.mla_heads),
        in_specs=[
            pl.BlockSpec((d.seq, MLA_HEAD_PAD), lambda b, h: (b, h)),
            pl.BlockSpec((d.seq, NOPE_DIM + MLA_V_DIM), lambda b, h: (b, h)),
            pl.BlockSpec((d.seq, LANES), lambda b, h: (b, 0)),
        ],
        out_specs=pl.BlockSpec((d.seq, MLA_V_DIM), lambda b, h: (b, h)),
        scratch_shapes=[pltpu.VMEM((d.seq, MLA_HEAD_PAD), BF16)],
        compiler_params=_cparams(("parallel", "parallel"), 48),
        name="mla_attention",
    )(q_full, kv, kpe)


def _swa_kernel(sink_ref, q_ref, k_ref, v_ref, o_ref, klo, khi, vlo, vhi, *, seq, window, heads_per_step):
    pair = pl.program_id(1)
    w = window
    lane = lax.broadcasted_iota(jnp.int32, (seq, LANES), 1)
    low = lane < SWA_HD
    for src_ref, lo_ref, hi_ref in ((k_ref, klo, khi), (v_ref, vlo, vhi)):
        x = src_ref[...].astype(F32)
        xs = pltpu.roll(x, SWA_HD, 1)
        zero = jnp.zeros_like(x)
        lo_ref[0] = jnp.where(low, x, zero).astype(BF16)
        hi_ref[0] = jnp.where(low, zero, xs).astype(BF16)
        lo_ref[1] = jnp.where(low, xs, zero).astype(BF16)
        hi_ref[1] = jnp.where(low, zero, x).astype(BF16)

    n_tiles = heads_per_step * SWA_HD // LANES
    tiles_per_kv = n_tiles // 2

    m_rows = tiles_per_kv * w
    row = lax.broadcasted_iota(jnp.int32, (m_rows, 1), 0)
    out_lane = lax.broadcasted_iota(jnp.int32, (m_rows, LANES), 1)

    def mask_bias(klen, is_first):
        qq = lax.broadcasted_iota(jnp.int32, (m_rows, klen), 0) % w
        kk = lax.broadcasted_iota(jnp.int32, (m_rows, klen), 1)
        valid = (kk <= qq) if is_first else ((kk > qq) & (kk <= qq + w))
        return jnp.where(valid, 0.0, NEG_INF).astype(F32)

    def sink_column(g, half):
        sink = jnp.zeros((m_rows, 1), F32)
        for ti in range(tiles_per_kv):
            head = pair * heads_per_step + 2 * (g * tiles_per_kv + ti) + half
            sink = jnp.where((row >= ti * w) & (row < (ti + 1) * w), sink_ref[head], sink)
        return sink

    sinks = [[sink_column(g, half) for half in range(2)] for g in range(2)]

    def block(r0, k0, klen, bias):
        for g in range(2):
            tiles = range(g * tiles_per_kv, (g + 1) * tiles_per_kv)
            q = jnp.concatenate([q_ref[pl.ds(r0, w), c * LANES:(c + 1) * LANES] for c in tiles], axis=0)
            kcat = jnp.concatenate([klo[g, pl.ds(k0, klen), :], khi[g, pl.ds(k0, klen), :]], axis=0)
            vcat = jnp.concatenate([vlo[g, pl.ds(k0, klen), :], vhi[g, pl.ds(k0, klen), :]], axis=0)
            s = _dot_nt(q, kcat)
            probs, inv_den = [], []
            for half in range(2):
                sink = sinks[g][half]
                sh = s[:, half * klen:(half + 1) * klen] + bias
                m = jnp.maximum(jnp.max(sh, axis=-1, keepdims=True), sink)
                p = jnp.exp(sh - m)
                inv_den.append(1.0 / (jnp.sum(p, axis=-1, keepdims=True) + jnp.exp(sink - m)))
                probs.append(p.astype(BF16))
            o = _dot(jnp.concatenate(probs, axis=1), vcat) * jnp.where(out_lane < SWA_HD, inv_den[0], inv_den[1])
            for ti, c in enumerate(tiles):
                o_ref[pl.ds(r0, w), c * LANES:(c + 1) * LANES] = o[ti * w:(ti + 1) * w].astype(o_ref.dtype)

    block(0, 0, w, mask_bias(w, True))
    band = mask_bias(2 * w, False)

    def body(n, carry):
        r0 = pl.multiple_of(n * w, w)
        block(r0, pl.multiple_of(r0 - w, w), 2 * w, band)
        return carry

    n_blk = seq // w
    lax.fori_loop(1, n_blk, body, 0, unroll=3 if (n_blk - 1) % 3 == 0 else 1)


def _swa_attention(z, sinks, d):
    t = d.tokens
    hps = 2 * (d.swa_heads // d.swa_kv_heads)
    qw = hps * SWA_HD
    n_pairs = _exact_div(d.swa_kv_heads, 2)
    jq, jk, jv = _exact_div(d.off_qs, qw), _exact_div(d.off_ks, LANES), _exact_div(d.off_vs, LANES)
    grid_spec = pltpu.PrefetchScalarGridSpec(
        num_scalar_prefetch=1,
        grid=(d.batch, n_pairs),
        in_specs=[
            pl.BlockSpec((d.seq, qw), lambda b, p, s: (b, jq + p)),
            pl.BlockSpec((d.seq, LANES), lambda b, p, s: (b, jk + p)),
            pl.BlockSpec((d.seq, LANES), lambda b, p, s: (b, jv + p)),
        ],
        out_specs=pl.BlockSpec((d.seq, qw), lambda b, p, s: (b, p)),
        scratch_shapes=[pltpu.VMEM((2, d.seq, LANES), BF16)] * 4,
    )
    return pl.pallas_call(
        functools.partial(_swa_kernel, seq=d.seq, window=d.window, heads_per_step=hps),
        out_shape=jax.ShapeDtypeStruct((t, d.swa_q), BF16),
        grid_spec=grid_spec,
        compiler_params=_cparams(("parallel", "parallel"), 48),
        name="swa_attention",
    )(sinks.astype(F32), z, z, z)


def _cross_router_kernel(h_ref, gc_ref, wq_ref, kvm_ref, wo_ref, gf_ref, wr_ref, br_ref,
                         h2_ref, xnf_ref, rt_ref, cnt_ref, run_ref, *, x_heads, n_groups, scale):
    tm = h_ref.shape[0]
    hx = x_heads * X_HD
    n_exp = n_groups * EPG

    @pl.when(pl.program_id(0) == 0)
    def _():
        run_ref[...] = jnp.zeros_like(run_ref)

    h = h_ref[...]
    hn = _rms(h, gc_ref[...]).astype(BF16)
    q = (_dot(hn, wq_ref[...]) * scale).astype(BF16)
    outs = []
    for hd in range(x_heads):
        kh = kvm_ref[:, hd * X_HD:(hd + 1) * X_HD]
        vh = kvm_ref[:, hx + hd * X_HD:hx + (hd + 1) * X_HD]
        s = _dot_nt(q[:, hd * X_HD:(hd + 1) * X_HD], kh)
        m = jnp.max(s, axis=-1, keepdims=True)
        p = jnp.exp(s - m)
        l = jnp.sum(p, axis=-1, keepdims=True)
        outs.append((_dot(p.astype(BF16), vh) / l).astype(BF16))
    h2 = h + _dot(jnp.concatenate(outs, axis=1), wo_ref[...])
    h2_ref[...] = h2
    xnf = _rms(h2, gf_ref[...])
    xnf_ref[...] = xnf

    logits = _dot(xnf.astype(BF16), wr_ref[...]) + br_ref[...]
    lane = lax.broadcasted_iota(jnp.int32, (tm, LANES), 1).astype(F32)
    big = float(LANES)
    is_group = (lane >= n_exp) & (lane < n_exp + n_groups)
    gl = jnp.where(is_group, logits, -jnp.inf)
    gmax = jnp.max(gl, axis=-1, keepdims=True)
    g_lane = jnp.min(jnp.where(gl == gmax, lane, big), axis=-1, keepdims=True)
    p_group = 1.0 / jnp.sum(jnp.where(is_group, jnp.exp(gl - gmax), 0.0), axis=-1, keepdims=True)
    e_lo = (g_lane - n_exp) * EPG
    in_group = (lane >= e_lo) & (lane < e_lo + EPG)
    el = jnp.where(in_group, logits, -jnp.inf)
    m1 = jnp.max(el, axis=-1, keepdims=True)
    i1 = jnp.min(jnp.where(el == m1, lane, big), axis=-1, keepdims=True)
    el2 = jnp.where(lane == i1, -jnp.inf, el)
    m2 = jnp.max(el2, axis=-1, keepdims=True)
    i2 = jnp.min(jnp.where(el2 == m2, lane, big), axis=-1, keepdims=True)
    w2 = jnp.exp(m2 - m1)
    gate1 = p_group / (1.0 + w2)
    gate2 = gate1 * w2

    hot1 = lane == i1
    hot2 = lane == i2
    onehot = jnp.where(hot1 | hot2, 1.0, 0.0)
    rr = lax.broadcasted_iota(jnp.int32, (tm, tm), 0)
    cc = lax.broadcasted_iota(jnp.int32, (tm, tm), 1)
    tri = jnp.where(cc < rr, 1.0, 0.0).astype(BF16)
    before = _dot(tri, onehot.astype(BF16)) + run_ref[...]
    rank1 = jnp.sum(jnp.where(hot1, before, 0.0), axis=-1, keepdims=True)
    rank2 = jnp.sum(jnp.where(hot2, before, 0.0), axis=-1, keepdims=True)
    run = run_ref[...] + jnp.sum(onehot, axis=0, keepdims=True)
    run_ref[...] = run
    cnt_ref[...] = run

    rt = jnp.where(lane == 0, i1, 0.0)
    rt = jnp.where(lane == 1, i2, rt)
    rt = jnp.where(lane == 2, gate1, rt)
    rt = jnp.where(lane == 3, gate2, rt)
    rt = jnp.where(lane == 4, rank1, rt)
    rt = jnp.where(lane == 5, rank2, rt)
    rt_ref[...] = rt


def _cross_router(h, g_cross, w_xq, kvm, w_xo, g_ffn, w_r, b_r, d):
    t = d.tokens
    tm = d.t_cross
    dm = d.d_model
    hx = d.x_heads * X_HD
    steps_per_batch = _exact_div(d.seq, tm)
    row = lambda i: (i, 0)
    fixed = lambda i: (0, 0)
    return pl.pallas_call(
        functools.partial(_cross_router_kernel, x_heads=d.x_heads, n_groups=d.n_groups, scale=X_HD ** -0.5),
        out_shape=(
            jax.ShapeDtypeStruct((t, dm), F32),
            jax.ShapeDtypeStruct((t, dm), F32),
            jax.ShapeDtypeStruct((t, LANES), F32),
            jax.ShapeDtypeStruct((1, LANES), F32),
        ),
        grid=(_exact_div(t, tm),),
        in_specs=[
            pl.BlockSpec((tm, dm), row),
            pl.BlockSpec((1, dm), fixed),
            pl.BlockSpec((dm, hx), fixed, pipeline_mode=pl.Buffered(1)),
            pl.BlockSpec((d.mem_len, 2 * hx), lambda i: (i // steps_per_batch, 0)),
            pl.BlockSpec((hx, dm), fixed, pipeline_mode=pl.Buffered(1)),
            pl.BlockSpec((1, dm), fixed),
            pl.BlockSpec((dm, LANES), fixed, pipeline_mode=pl.Buffered(1)),
            pl.BlockSpec((1, LANES), fixed),
        ],
        out_specs=(
            pl.BlockSpec((tm, dm), row),
            pl.BlockSpec((tm, dm), row),
            pl.BlockSpec((tm, LANES), row),
            pl.BlockSpec((1, LANES), fixed),
        ),
        scratch_shapes=[pltpu.VMEM((1, LANES), F32)],
        compiler_params=_cparams(("arbitrary",), 56),
        name="cross_attention_router",
    )(h, g_cross.reshape(1, dm), w_xq, kvm, w_xo, g_ffn.reshape(1, dm), w_r, b_r)


def _row_copy(src_hbm, src_row, dst_ref, dst_row, sem):
    return pltpu.make_async_copy(src_hbm.at[pl.ds(src_row, 1)], dst_ref.at[pl.ds(dst_row, 1)], sem)


def _dispatch_kernel(dest_ref, fill_ref, x_ref, xs_hbm, zero_ref, sem, zrow_sem, zblk_sem, *,
                     t_tok, n_experts, moe_block, n_blocks):
    base = pl.program_id(0) * t_tok

    @pl.when(pl.program_id(0) == 0)
    def _():
        zero_ref[...] = jnp.zeros_like(zero_ref)
        zero_block = lambda b: pltpu.make_async_copy(
            zero_ref, xs_hbm.at[pl.ds(pl.multiple_of(b * moe_block, moe_block), moe_block)], zblk_sem)

        def for_fill_rows(fn):
            for e in range(n_experts):
                lax.fori_loop(fill_ref[2 * e], fill_ref[2 * e + 1], fn, 0)

        def start_row(r, carry):
            _row_copy(zero_ref, 0, xs_hbm, r, zrow_sem).start()
            return carry

        def wait_row(r, carry):
            _row_copy(zero_ref, 0, xs_hbm, 0, zrow_sem).wait()
            return carry

        def start_block(b, carry):
            zero_block(b).start()
            return carry

        def wait_block(b, carry):
            zero_block(0).wait()
            return carry

        for_fill_rows(start_row)
        lax.fori_loop(fill_ref[2 * n_experts], n_blocks, start_block, 0)
        for_fill_rows(wait_row)
        lax.fori_loop(fill_ref[2 * n_experts], n_blocks, wait_block, 0)

    def issue(r, carry):
        for k in range(TOP_K):
            _row_copy(x_ref, r, xs_hbm, dest_ref[TOP_K * (base + r) + k], sem).start(priority=k % 2)
        return carry

    lax.fori_loop(0, t_tok, issue, 0, unroll=8)
    for k in range(TOP_K):
        pltpu.make_async_copy(x_ref, x_ref, sem).wait()


def _dispatch(dest, fill, x, d):
    t = d.tokens
    p_rows = d.n_blocks * d.moe_block
    grid_spec = pltpu.PrefetchScalarGridSpec(
        num_scalar_prefetch=2,
        grid=(_exact_div(t, d.t_tok),),
        in_specs=[pl.BlockSpec((d.t_tok, d.d_model), lambda i, dest, fill: (i, 0))],
        out_specs=pl.BlockSpec(memory_space=pl.ANY),
        scratch_shapes=[pltpu.VMEM((d.moe_block, d.d_model), x.dtype)] + [pltpu.SemaphoreType.DMA(())] * 3,
    )
    return pl.pallas_call(
        functools.partial(_dispatch_kernel, t_tok=d.t_tok, n_experts=d.n_experts, moe_block=d.moe_block,
                          n_blocks=d.n_blocks),
        out_shape=jax.ShapeDtypeStruct((p_rows, d.d_model), x.dtype),
        grid_spec=grid_spec,
        compiler_params=_cparams(("arbitrary",), 32),
        name="moe_dispatch",
    )(dest, fill, x)


W_SLABS = 4


def _moe_steps(d):
    return W_SLABS + d.n_blocks + W_SLABS * d.n_experts + 1


def _moe_schedule(padded, d):
    bm, ne, nblk = d.moe_block, d.n_experts, d.n_blocks
    i32 = jnp.int32
    nb = padded // bm
    first_blk = (jnp.cumsum(padded) - padded) // bm
    n_used = jnp.sum(nb)
    per_expert = jnp.where(jnp.arange(ne) == ne - 1, nb, jnp.maximum(nb, W_SLABS))
    step_end = W_SLABS + jnp.cumsum(per_expert)
    step_start = step_end - per_expert
    total = step_end[-1]
    s = jnp.arange(_moe_steps(d), dtype=i32)
    is_pro = s < W_SLABS
    is_tail = s >= total
    e = jnp.minimum(jnp.sum((step_end[None, :] <= s[:, None]).astype(i32), axis=1), ne - 1)
    j = jnp.where(is_pro, s, s - step_start[e])
    has_blk = jnp.logical_not(is_pro | is_tail) & (j < nb[e])
    blk = jnp.where(is_pro, 0, first_blk[e] + jnp.minimum(j, nb[e]))
    xblk = jnp.where(has_blk, blk, 0)
    oblk = jnp.where(is_tail, jnp.minimum(n_used + s - total, nblk), blk)
    nxt = jnp.where(is_pro, 0, jnp.minimum(e + 1, ne - 1))
    slab = jnp.where(is_tail, W_SLABS - 1, jnp.minimum(j, W_SLABS - 1))
    slot = jnp.where(is_pro, 1, e % 2)
    return tuple(v.astype(i32) for v in (xblk, oblk, nxt, slab, slot, is_tail))


def _silu_mul(a, b):
    return (a / (1.0 + jnp.exp(-a))) * b


def _expert_up_kernel(xblk, oblk, nxt, slab, slot, tail, x_ref, wg_st, wu_st, hb_ref, wg0, wg1, wu0, wu1):
    del xblk, oblk, nxt
    s = pl.program_id(0)
    rows = wg_st.shape[1]

    @pl.when(s == 0)
    def _():
        wg1[...] = jnp.zeros_like(wg1)
        wu1[...] = jnp.zeros_like(wu1)

    @pl.when(tail[s] == 1)
    def _():
        hb_ref[...] = jnp.zeros_like(hb_ref)

    r0 = pl.multiple_of(slab[s] * rows, rows)
    for cur, (wg_c, wu_c, wg_n, wu_n) in enumerate(((wg0, wu0, wg1, wu1), (wg1, wu1, wg0, wu0))):
        @pl.when((tail[s] == 0) & (slot[s] == cur))
        def _():
            wg_n[pl.ds(r0, rows), :] = wg_st[0].astype(BF16)
            wu_n[pl.ds(r0, rows), :] = wu_st[0].astype(BF16)
            x = x_ref[...].astype(BF16)
            hb_ref[...] = _silu_mul(_dot(x, wg_c[...]), _dot(x, wu_c[...])).astype(hb_ref.dtype)


def _expert_down_kernel(xblk, oblk, nxt, slab, slot, tail, hb_ref, wd_st, y_ref, wd0, wd1):
    del xblk, oblk, nxt
    s = pl.program_id(0)
    rows = wd_st.shape[1]

    @pl.when(s == 0)
    def _():
        wd1[...] = jnp.zeros_like(wd1)

    @pl.when(tail[s] == 1)
    def _():
        y_ref[...] = jnp.zeros_like(y_ref)

    r0 = pl.multiple_of(slab[s] * rows, rows)
    for cur, (wd_c, wd_n) in enumerate(((wd0, wd1), (wd1, wd0))):
        @pl.when((tail[s] == 0) & (slot[s] == cur))
        def _():
            wd_n[pl.ds(r0, rows), :] = wd_st[0].astype(BF16)
            y_ref[...] = _dot(hb_ref[...], wd_c[...])


def _experts(sched, xs, w_gate, w_up, w_down, layer, d):
    bm = d.moe_block
    dm, de = d.d_model, d.d_expert
    ne = d.n_experts
    rows_out = (d.n_blocks + 1) * bm
    w_gate, w_up = w_gate.reshape(-1, dm, de), w_up.reshape(-1, dm, de)
    w_down = w_down.reshape(-1, de, dm)
    x_map = lambda s, xb, ob, nx, sl, st, tl: (xb[s], 0)
    o_map = lambda s, xb, ob, nx, sl, st, tl: (ob[s], 0)
    w_map = lambda s, xb, ob, nx, sl, st, tl: (layer * ne + nx[s], sl[s], 0)
    ku, kd = _exact_div(dm, W_SLABS), _exact_div(de, W_SLABS)
    hb = pl.pallas_call(
        _expert_up_kernel,
        out_shape=jax.ShapeDtypeStruct((rows_out, de), BF16),
        grid_spec=pltpu.PrefetchScalarGridSpec(
            num_scalar_prefetch=len(sched),
            grid=(_moe_steps(d),),
            in_specs=[pl.BlockSpec((bm, dm), x_map), pl.BlockSpec((1, ku, de), w_map),
                      pl.BlockSpec((1, ku, de), w_map)],
            out_specs=pl.BlockSpec((bm, de), o_map),
            scratch_shapes=[pltpu.VMEM((dm, de), BF16)] * 4,
        ),
        compiler_params=_cparams(("arbitrary",), 56),
        name="moe_experts_up",
    )(*sched, xs, w_gate, w_up)
    return pl.pallas_call(
        _expert_down_kernel,
        out_shape=jax.ShapeDtypeStruct((rows_out, dm), F32),
        grid_spec=pltpu.PrefetchScalarGridSpec(
            num_scalar_prefetch=len(sched),
            grid=(_moe_steps(d),),
            in_specs=[pl.BlockSpec((bm, de), x_map), pl.BlockSpec((1, kd, dm), w_map)],
            out_specs=pl.BlockSpec((bm, dm), o_map),
            scratch_shapes=[pltpu.VMEM((de, dm), BF16)] * 2,
        ),
        compiler_params=_cparams(("arbitrary",), 40),
        name="moe_experts_down",
    )(*sched, hb, w_down)


def _combine_kernel(dest_ref, y_hbm, h_ref, rt_ref, g_ref, *rest, t_tok, last):
    if last:
        out_ref, ybuf, sem = rest
    else:
        h3_ref, out_ref, ybuf, sem = rest
    i = pl.program_id(0)
    buf = i % 2

    def gather(step, b):
        base = step * t_tok

        def issue(r, carry):
            for k in range(TOP_K):
                _row_copy(y_hbm, dest_ref[TOP_K * (base + r) + k], ybuf.at[b, k], r, sem.at[b]).start(priority=k % 2)
            return carry

        lax.fori_loop(0, t_tok, issue, 0, unroll=8)

    @pl.when(i == 0)
    def _():
        gather(0, 0)

    @pl.when(i + 1 < pl.num_programs(0))
    def _():
        gather(i + 1, 1 - buf)

    pltpu.make_async_copy(ybuf.at[buf], ybuf.at[buf], sem.at[buf]).wait()
    rt = rt_ref[...]
    h3 = h_ref[...] + rt[:, 2:3] * ybuf[buf, 0] + rt[:, 3:4] * ybuf[buf, 1]
    if not last:
        h3_ref[...] = h3
    out_ref[...] = _rms(h3, g_ref[...]).astype(out_ref.dtype)


def _combine(dest, y, h2, rt, g_next, d, last):
    t = d.tokens
    tt = d.t_tok
    dm = d.d_model
    row = lambda i, ds: (i, 0)
    out_specs = pl.BlockSpec((tt, dm), row)
    if last:
        out_shape = jax.ShapeDtypeStruct((t, dm), F32)
    else:
        out_shape = (jax.ShapeDtypeStruct((t, dm), F32), jax.ShapeDtypeStruct((t, dm), BF16))
        out_specs = (out_specs, pl.BlockSpec((tt, dm), row))
    grid_spec = pltpu.PrefetchScalarGridSpec(
        num_scalar_prefetch=1,
        grid=(_exact_div(t, tt),),
        in_specs=[
            pl.BlockSpec(memory_space=pl.ANY),
            pl.BlockSpec((tt, dm), row),
            pl.BlockSpec((tt, LANES), row),
            pl.BlockSpec((1, dm), lambda i, ds: (0, 0)),
        ],
        out_specs=out_specs,
        scratch_shapes=[pltpu.VMEM((2, TOP_K, tt, dm), F32), pltpu.SemaphoreType.DMA((2,))],
    )
    return pl.pallas_call(
        functools.partial(_combine_kernel, t_tok=tt, last=last),
        out_shape=out_shape,
        grid_spec=grid_spec,
        compiler_params=_cparams(("arbitrary",), 48),
        name="moe_combine",
    )(dest, y, h2, rt, g_next.reshape(1, dm))


def _layer(h, xn, cos, sin, memn, p, d, g_next, last):
    t = d.tokens
    dm = d.d_model
    tm = d.tm
    row128 = pl.BlockSpec((tm, LANES), lambda i, j: (i, 0))

    tn = d.tn_in
    row128_in = pl.BlockSpec((d.tm_in, LANES), lambda i, j: (i, 0))
    z = _matmul(
        xn, p["w_main"], k=dm, a_col=0, tm=d.tm_in, tn=tn, out_dtype=BF16, vmem_mib=60,
        post=functools.partial(_post_in_proj, j_rope0=_exact_div(d.off_qs, tn), j_k=_exact_div(d.off_ks, tn),
                               j_rope1=_exact_div(d.off_vs, tn), tn=tn, q_scale=SWA_HD ** -0.5),
        extras=((cos, row128_in), (sin, row128_in)), name="in_proj")
    kpe = _matmul(xn, p["w_kr"], k=dm, a_col=0, tm=tm, tn=LANES, out_dtype=BF16, post=_post_rope_all,
                  extras=((cos, row128), (sin, row128)), name="rope_key_proj")

    tn_q = min(4, d.mla_heads) * MLA_HEAD_PAD
    q_full = _matmul(
        z, p["w_uq"], k=d.q_lora, a_col=0, tm=tm, tn=tn_q, out_dtype=BF16, pre=_pre_rms,
        post=functools.partial(_post_mla_q, scale=(NOPE_DIM + ROPE_DIM) ** -0.5),
        extras=((p["g_qa"], pl.BlockSpec((1, d.q_lora), lambda i, j: (0, 0))), (cos, row128), (sin, row128)),
        name="mla_q_proj")
    kv = _matmul(
        z, p["w_ukv"], k=d.kv_lora, a_col=_exact_div(d.off_ckv, d.kv_lora), tm=tm, tn=tn_q, out_dtype=BF16,
        pre=_pre_rms, post=_post_cast,
        extras=((p["g_kva"], pl.BlockSpec((1, d.kv_lora), lambda i, j: (0, 0))),), name="mla_kv_proj")
    o_a = _mla_attention(q_full, kv, kpe, d)

    o_b = _swa_attention(z, p["sinks"], d)

    merged = _merge(o_a, p["w_pa"], o_b, p["w_pb"], z, d)
    h1 = _matmul(merged, p["w_o"], k=dm, a_col=0, tm=tm, tn=d.tn, out_dtype=F32, post=_post_residual,
                 extras=((h, pl.BlockSpec((tm, d.tn), lambda i, j: (i, j))),), name="out_proj")

    kvm = _matmul(memn, p["w_xkv"], k=dm, a_col=0, tm=memn.shape[0], tn=d.tn, out_dtype=BF16,
                  post=_post_cast, name="mem_kv_proj")
    h2, xnf, rt, cnt = _cross_router(h1, p["g_cross"], p["w_xq"], kvm, p["w_xo"], p["g_ffn"],
                                     p["w_r"], p["b_r"], d)

    bm = d.moe_block
    counts = cnt[0, :d.n_experts].astype(jnp.int32)
    padded = (counts + bm - 1) // bm * bm
    pad_ends = jnp.cumsum(padded)
    pad_starts = pad_ends - padded
    expert = rt[:, 0:TOP_K].astype(jnp.int32)
    dest = (pad_starts[expert] + rt[:, 4:4 + TOP_K].astype(jnp.int32)).reshape(t * TOP_K)
    fill = jnp.concatenate([jnp.stack([pad_starts + counts, pad_ends], axis=1).reshape(-1),
                            pad_ends[-1:] // bm]).astype(jnp.int32)

    xs = _dispatch(dest, fill, xnf, d)
    y = _experts(_moe_schedule(padded, d), xs, p["w_gate"], p["w_up"], p["w_down"], p["layer"], d)
    return _combine(dest, y, h2, rt, g_next, d, last)


def _prep_layer(l, d, g_mix, w_in, g_qa, g_kva, w_uq, w_ukv, sinks, w_pa, w_pb, w_o, g_cross, w_xq, w_xkv,
                w_xo, g_ffn, w_group, b_group, w_router, b_router, w_gate, w_up, w_down):
    dm = d.d_model
    kr0 = d.q_lora + d.kv_lora
    w = w_in[l]
    w_main = jnp.concatenate([w[:, :kr0].astype(BF16), w[:, kr0 + ROPE_DIM:].astype(BF16)], axis=1)
    w_kr = jnp.pad(w[:, kr0:kr0 + ROPE_DIM], ((0, 0), (0, LANES - ROPE_DIM))).astype(BF16)
    qk = NOPE_DIM + ROPE_DIM
    wq = w_uq[l].reshape(d.q_lora, d.mla_heads, qk)
    wq = jnp.pad(wq, ((0, 0), (0, 0), (0, MLA_HEAD_PAD - qk))).reshape(d.q_lora, d.mla_heads * MLA_HEAD_PAD)
    n_r = d.n_experts + d.n_groups
    w_r = jnp.pad(jnp.concatenate([w_router[l], w_group[l]], axis=1), ((0, 0), (0, LANES - n_r)))
    b_r = jnp.pad(jnp.concatenate([b_router[l], b_group[l]]), (0, LANES - n_r)).reshape(1, LANES)
    return dict(
        w_main=w_main, w_kr=w_kr, w_uq=wq.astype(BF16), w_ukv=w_ukv[l].astype(BF16),
        g_qa=g_qa[l].reshape(1, -1).astype(F32), g_kva=g_kva[l].reshape(1, -1).astype(F32),
        sinks=sinks[l], w_pa=w_pa[l].astype(BF16), w_pb=w_pb[l].astype(BF16), w_o=w_o[l].astype(BF16),
        g_cross=g_cross[l].astype(F32), w_xq=w_xq[l].astype(BF16), w_xkv=w_xkv[l].astype(BF16),
        w_xo=w_xo[l].astype(BF16), g_ffn=g_ffn[l].astype(F32), w_r=w_r.astype(BF16), b_r=b_r.astype(F32),
        w_gate=w_gate, w_up=w_up, w_down=w_down, layer=l,
    )


def _forward(d, x, mem, positions, g_mix, w_in, g_qa, g_kva, w_uq, w_ukv, sinks, w_pa, w_pb, w_o,
             g_cross, g_mem, w_xq, w_xkv, w_xo, g_ffn, w_group, b_group, w_router, b_router,
             w_gate, w_up, w_down, g_final):
    depth = w_in.shape[0]
    t = d.tokens
    dm = d.d_model
    cos, sin = _rope_tables(positions, d)
    memn = _rmsnorm(mem.reshape(d.batch * d.mem_len, dm), g_mem, BF16, d.t_norm)
    h = x.reshape(t, dm)
    xn = _rmsnorm(h, g_mix[0], BF16, d.t_norm)
    for l in range(depth):
        p = _prep_layer(l, d, g_mix, w_in, g_qa, g_kva, w_uq, w_ukv, sinks, w_pa, w_pb, w_o, g_cross,
                        w_xq, w_xkv, w_xo, g_ffn, w_group, b_group, w_router, b_router, w_gate, w_up, w_down)
        last = l == depth - 1
        g_next = g_final if last else g_mix[l + 1]
        res = _layer(h, xn, cos, sin, memn, p, d, g_next, last)
        if last:
            return res.reshape(d.batch, d.seq, dm)
        h, xn = res


def kernel(x, mem, positions, g_mix, w_in, g_qa, g_kva, w_uq, w_ukv, sinks, w_pa, w_pb, w_o, g_cross, g_mem,
           w_xq, w_xkv, w_xo, g_ffn, w_group, b_group, w_router, b_router, w_gate, w_up, w_down, g_final):
    return _forward(Dims(), x, mem, positions, g_mix, w_in, g_qa, g_kva, w_uq, w_ukv, sinks, w_pa, w_pb, w_o,
                    g_cross, g_mem, w_xq, w_xkv, w_xo, g_ffn, w_group, b_group, w_router, b_router,
                    w_gate, w_up, w_down, g_final)
```
